```python
import functools
import jax, jax.numpy as jnp
from jax import lax
import numpy as np

D_MODEL = 1024
BATCH = 16
SEQ = 256
DEPTH = 2
DEC_BATCH = 8
DEC_SEQ = 4096
PAST_LEN = 512

GRID_W = 64
HEAD_DIM = 64
HQ_ATTN = D_MODEL // (2 * HEAD_DIM)
KV_ATTN = HQ_ATTN // 4
HQ_SWA = D_MODEL // (2 * HEAD_DIM)
KV_SWA = HQ_SWA // 4
WINDOW = 128
Q_BLOCK = 128
ROPE_THETA = 10000.0
RET_DK = 256
RET_DV = 512
RET_HEADS = D_MODEL // RET_DK
RET_CHUNK = 128
D_FF = 4 * D_MODEL
EPS = 1e-6
NEG_INF = -1e30
ADA_CHUNKS = 6

ATTN_Q_W = HQ_ATTN * HEAD_DIM
ATTN_KV_W = KV_ATTN * HEAD_DIM
SWA_Q_W = HQ_SWA * HEAD_DIM
SWA_KV_W = KV_SWA * HEAD_DIM
ATTN_SPLITS = [ATTN_Q_W, ATTN_Q_W + ATTN_KV_W, ATTN_Q_W + 2 * ATTN_KV_W,
               ATTN_Q_W + 2 * ATTN_KV_W + SWA_Q_W, ATTN_Q_W + 2 * ATTN_KV_W + SWA_Q_W + SWA_KV_W]
ATTN_IN_W = ATTN_Q_W + 2 * ATTN_KV_W + SWA_Q_W + 2 * SWA_KV_W
ATTN_OUT_W = ATTN_Q_W + SWA_Q_W
RET_QK_W = RET_HEADS * RET_DK
RET_V_W = RET_HEADS * RET_DV
RET_SPLITS = [RET_QK_W, 2 * RET_QK_W, 2 * RET_QK_W + RET_V_W]
RET_IN_W = 2 * RET_QK_W + 2 * RET_V_W

kernel_name = 'hybrid_diffusion_prefix_step'


def _rms(x, gain):
    x32 = x.astype(jnp.float32)
    y = x32 * lax.rsqrt(jnp.mean(x32 * x32, axis=-1, keepdims=True) + EPS)
    return (y * gain.astype(jnp.float32)).astype(x.dtype)


def _rope_1d(x, ang):
    cos = jnp.cos(ang)[None, :, None, :]
    sin = jnp.sin(ang)[None, :, None, :]
    x1, x2 = jnp.split(x, 2, axis=-1)
    return jnp.concatenate([x1 * cos - x2 * sin, x1 * sin + x2 * cos], axis=-1)


def _rope_2d(x):
    seq_len, d = x.shape[1], x.shape[-1]
    rows = seq_len // GRID_W
    row = jnp.repeat(jnp.arange(rows, dtype=jnp.float32), GRID_W)
    col = jnp.tile(jnp.arange(GRID_W, dtype=jnp.float32), rows)
    n_freq = d // 4
    inv = ROPE_THETA ** (-jnp.arange(n_freq, dtype=jnp.float32) / n_freq)
    x32 = x.astype(jnp.float32)
    half = d // 2
    xr = _rope_1d(x32[..., :half], row[:, None] * inv[None, :])
    xc = _rope_1d(x32[..., half:], col[:, None] * inv[None, :])
    return jnp.concatenate([xr, xc], axis=-1).astype(x.dtype)


def _attend(qb, k, v, sink=None, valid=None):
    s = jnp.einsum('bqhgd,bkhd->bhgqk', qb, k).astype(jnp.float32) * (HEAD_DIM ** -0.5)
    if valid is not None:
        s = jnp.where(valid, s, NEG_INF)
    m = jnp.max(s, axis=-1, keepdims=True)
    if sink is None:
        p = jnp.exp(s - m)
        denom = jnp.sum(p, axis=-1, keepdims=True)
    else:
        sk = sink.astype(jnp.float32)[None, :, :, None, None]
        m = jnp.maximum(m, sk)
        p = jnp.exp(s - m)
        denom = jnp.sum(p, axis=-1, keepdims=True) + jnp.exp(sk - m)
    p = (p / denom).astype(v.dtype)
    return jnp.einsum('bhgqk,bkhd->bqhgd', p, v)


def _blocked(q, n_kv, fn):
    bsz, seq_len, hq, d = q.shape
    nb = seq_len // Q_BLOCK
    qb = q.reshape(bsz, nb, Q_BLOCK, n_kv, hq // n_kv, d).transpose(1, 0, 2, 3, 4, 5)
    out = lax.map(lambda a: fn(a[0], a[1]), (qb, jnp.arange(nb)))
    return out.transpose(1, 0, 2, 3, 4, 5).reshape(bsz, seq_len, hq * d)


def _retention_scan(q, k, v, log_gamma, s0):
    bsz, seq_len, n_h, _ = q.shape
    dv = v.shape[-1]
    nc = seq_len // RET_CHUNK

    def chunks(t):
        return t.astype(jnp.float32).reshape(bsz, nc, RET_CHUNK, n_h, t.shape[-1]).transpose(1, 0, 3, 2, 4)

    pos = jnp.arange(RET_CHUNK, dtype=jnp.float32)
    diff = pos[:, None] - pos[None, :]
    lg = log_gamma[:, None, None]
    decay = jnp.exp(jnp.where(diff[None] >= 0, diff[None] * lg, NEG_INF))
    q_decay = jnp.exp((pos + 1.0)[None, :] * log_gamma[:, None])[..., None]
    k_decay = jnp.exp((RET_CHUNK - 1.0 - pos)[None, :] * log_gamma[:, None])[..., None]
    chunk_decay = jnp.exp(RET_CHUNK * log_gamma)[:, None, None]

    def step(s, xs):
        qc, kc, vc = xs
        att = jnp.einsum('bhid,bhjd->bhij', qc, kc) * decay
        o = jnp.einsum('bhij,bhje->bhie', att, vc) + jnp.einsum('bhid,bhde->bhie', qc * q_decay, s)
        s = chunk_decay * s + jnp.einsum('bhjd,bhje->bhde', kc * k_decay, vc)
        return s, o

    s_final, o = lax.scan(step, s0.astype(jnp.float32), (chunks(q), chunks(k), chunks(v)))
    o = o.transpose(1, 0, 3, 2, 4).reshape(bsz, seq_len, n_h, dv)
    return o.astype(v.dtype), s_final.astype(s0.dtype)


def _attn_mixer(h, w_in, q_gain, k_gain, sink, w_out, ctx):
    bsz, seq_len, _ = h.shape
    qa, ka, va, qs, ks, vs = jnp.split(h @ w_in, ATTN_SPLITS, axis=-1)
    qa = _rms(qa.reshape(bsz, seq_len, HQ_ATTN, HEAD_DIM), q_gain)
    ka = _rms(ka.reshape(bsz, seq_len, KV_ATTN, HEAD_DIM), k_gain)
    va = va.reshape(bsz, seq_len, KV_ATTN, HEAD_DIM)
    qs = qs.reshape(bsz, seq_len, HQ_SWA, HEAD_DIM)
    ks = ks.reshape(bsz, seq_len, KV_SWA, HEAD_DIM)
    vs = vs.reshape(bsz, seq_len, KV_SWA, HEAD_DIM)
    sink_g = sink.reshape(KV_SWA, HQ_SWA // KV_SWA)
    if ctx is None:
        out_a = _blocked(qa, KV_ATTN, lambda qb, idx: _attend(qb, ka, va))
        out_s = _blocked(qs, KV_SWA, lambda qb, idx: _attend(qb, ks, vs, sink=sink_g))
        state = (ka, va, ks, vs)
    else:
        ka_ctx, va_ctx, ks_ctx, vs_ctx = ctx
        qa, ka, qs, ks = _rope_2d(qa), _rope_2d(ka), _rope_2d(qs), _rope_2d(ks)
        k_all = jnp.concatenate([ka_ctx, ka], axis=1)
        v_all = jnp.concatenate([va_ctx, va], axis=1)
        out_a = _blocked(qa, KV_ATTN, lambda qb, idx: _attend(qb, k_all, v_all))
        pad = ((0, 0), (WINDOW, WINDOW), (0, 0), (0, 0))
        ks_pad = jnp.pad(ks, pad)
        vs_pad = jnp.pad(vs, pad)
        band = Q_BLOCK + 2 * WINDOW
        ctx_valid = jnp.ones((Q_BLOCK, ks_ctx.shape[1]), dtype=bool)

        def band_block(qb, idx):
            start = idx * Q_BLOCK
            kb = lax.dynamic_slice_in_dim(ks_pad, start, band, axis=1)
            vb = lax.dynamic_slice_in_dim(vs_pad, start, band, axis=1)
            q_pos = start + jnp.arange(Q_BLOCK)
            k_pos = start - WINDOW + jnp.arange(band)
            valid = ((jnp.abs(q_pos[:, None] - k_pos[None, :]) <= WINDOW)
                     & (k_pos >= 0)[None, :] & (k_pos < seq_len)[None, :])
            return _attend(qb, jnp.concatenate([ks_ctx, kb], axis=1), jnp.concatenate([vs_ctx, vb], axis=1),
                           sink=sink_g, valid=jnp.concatenate([ctx_valid, valid], axis=1))

        out_s = _blocked(qs, KV_SWA, band_block)
        state = ()
    return jnp.concatenate([out_a, out_s], axis=-1) @ w_out, state


def _ret_mixer(h, w_in, decay_fwd, decay_bwd, gn_gain, w_out, ctx):
    bsz, seq_len, _ = h.shape
    q, k, v, g = jnp.split(h @ w_in, RET_SPLITS, axis=-1)
    q = q.reshape(bsz, seq_len, RET_HEADS, RET_DK)
    k = k.reshape(bsz, seq_len, RET_HEADS, RET_DK)
    v = v.reshape(bsz, seq_len, RET_HEADS, RET_DV)
    if ctx is None:
        s0_f = jnp.zeros((bsz, RET_HEADS, RET_DK, RET_DV), h.dtype)
        s0_b = jnp.zeros((bsz, RET_HEADS, RET_DK, RET_DV), h.dtype)
    else:
        q, k = _rope_2d(q), _rope_2d(k)
        s0_f, s0_b = ctx
    k = k * (RET_DK ** -0.5)
    o_f, s_f = _retention_scan(q, k, v, jax.nn.log_sigmoid(decay_fwd.astype(jnp.float32)), s0_f)
    o_b, s_b = _retention_scan(jnp.flip(q, 1), jnp.flip(k, 1), jnp.flip(v, 1),
                               jax.nn.log_sigmoid(decay_bwd.astype(jnp.float32)), s0_b)
    o = _rms(o_f + jnp.flip(o_b, 1), gn_gain).reshape(bsz, seq_len, RET_V_W)
    out = (jax.nn.silu(g) * o) @ w_out
    state = (s_f, s_b) if ctx is None else ()
    return out, state


def _layer(x, cond, ada_w, ada_b, norm_mix, norm_mlp, mlp_w1, mlp_w2, mixer):
    mod = jax.nn.silu(cond) @ ada_w + ada_b
    sh1, sc1, g1, sh2, sc2, g2 = jnp.split(mod[..., None, :], ADA_CHUNKS, axis=-1)
    y, state = mixer(_rms(x, norm_mix) * (1.0 + sc1) + sh1)
    x = x + g1 * y
    h = _rms(x, norm_mlp) * (1.0 + sc2) + sh2
    x = x + g2 * (jnp.square(jax.nn.relu(h @ mlp_w1)) @ mlp_w2)
    return x, state


def setup_inputs(seed: int = 0) -> dict:
    key = jax.random.key(seed)
    keys = jax.random.split(key, 40)
    counter = [0]

    def nrm(shape, scale=1.0):
        sub = keys[counter[0]]
        counter[0] += 1
        return scale * jax.random.normal(sub, shape, jnp.float32)

    def gain(shape):
        return 1.0 + 0.1 * nrm(shape)

    decay_init = jnp.log(2.0 ** (5.0 + jnp.arange(RET_HEADS, dtype=jnp.float32)) - 1.0)
    return {
        'x_prompt': nrm((BATCH, SEQ, D_MODEL)),
        'x_sample': nrm((DEC_BATCH, DEC_SEQ, D_MODEL)),
        'c': nrm((DEC_BATCH, D_MODEL)),
        'cache_l0_attn_k': nrm((DEC_BATCH, PAST_LEN, KV_ATTN, HEAD_DIM)),
        'cache_l0_attn_v': nrm((DEC_BATCH, PAST_LEN, KV_ATTN, HEAD_DIM)),
        'cache_l0_swa_k': nrm((DEC_BATCH, PAST_LEN, KV_SWA, HEAD_DIM)),
        'cache_l0_swa_v': nrm((DEC_BATCH, PAST_LEN, KV_SWA, HEAD_DIM)),
        'state_l1_ret_fwd': nrm((DEC_BATCH, RET_HEADS, RET_DK, RET_DV), 0.5),
        'state_l1_ret_bwd': nrm((DEC_BATCH, RET_HEADS, RET_DK, RET_DV), 0.5),
        'c_ctx': nrm((D_MODEL,)),
        'l0_ada_w': nrm((D_MODEL, ADA_CHUNKS * D_MODEL), 0.5 * D_MODEL ** -0.5),
        'l0_ada_b': nrm((ADA_CHUNKS * D_MODEL,), 0.02),
        'l0_norm_mix': gain((D_MODEL,)),
        'l0_norm_mlp': gain((D_MODEL,)),
        'l0_w_in': nrm((D_MODEL, ATTN_IN_W), D_MODEL ** -0.5),
        'l0_q_norm': gain((HEAD_DIM,)),
        'l0_k_norm': gain((HEAD_DIM,)),
        'l0_sink': nrm((HQ_SWA,), 0.5),
        'l0_w_out': nrm((ATTN_OUT_W, D_MODEL), ATTN_OUT_W ** -0.5),
        'l0_mlp_w1': nrm((D_MODEL, D_FF), D_MODEL ** -0.5),
        'l0_mlp_w2': nrm((D_FF, D_MODEL), D_FF ** -0.5),
        'l1_ada_w': nrm((D_MODEL, ADA_CHUNKS * D_MODEL), 0.5 * D_MODEL ** -0.5),
        'l1_ada_b': nrm((ADA_CHUNKS * D_MODEL,), 0.02),
        'l1_norm_mix': gain((D_MODEL,)),
        'l1_norm_mlp': gain((D_MODEL,)),
        'l1_w_in': nrm((D_MODEL, RET_IN_W), D_MODEL ** -0.5),
        'l1_ret_decay_fwd': decay_init + nrm((RET_HEADS,), 0.1),
        'l1_ret_decay_bwd': decay_init + nrm((RET_HEADS,), 0.1),
        'l1_ret_gn': gain((RET_HEADS, RET_DV)),
        'l1_w_out': nrm((RET_V_W, D_MODEL), RET_V_W ** -0.5),
        'l1_mlp_w1': nrm((D_MODEL, D_FF), D_MODEL ** -0.5),
        'l1_mlp_w2': nrm((D_FF, D_MODEL), D_FF ** -0.5),
        'final_norm': gain((D_MODEL,)),
    }


def reference(x_prompt, x_sample, c, cache_l0_attn_k, cache_l0_attn_v, cache_l0_swa_k, cache_l0_swa_v,
              state_l1_ret_fwd, state_l1_ret_bwd, c_ctx,
              l0_ada_w, l0_ada_b, l0_norm_mix, l0_norm_mlp, l0_w_in, l0_q_norm, l0_k_norm, l0_sink, l0_w_out,
              l0_mlp_w1, l0_mlp_w2,
              l1_ada_w, l1_ada_b, l1_norm_mix, l1_norm_mlp, l1_w_in, l1_ret_decay_fwd, l1_ret_decay_bwd,
              l1_ret_gn, l1_w_out, l1_mlp_w1, l1_mlp_w2, final_norm):
    layer_common = [
        (l0_ada_w, l0_ada_b, l0_norm_mix, l0_norm_mlp, l0_mlp_w1, l0_mlp_w2),
        (l1_ada_w, l1_ada_b, l1_norm_mix, l1_norm_mlp, l1_mlp_w1, l1_mlp_w2),
    ]
    layer_mixer = [
        functools.partial(_attn_mixer, w_in=l0_w_in, q_gain=l0_q_norm, k_gain=l0_k_norm,
                          sink=l0_sink, w_out=l0_w_out),
        functools.partial(_ret_mixer, w_in=l1_w_in, decay_fwd=l1_ret_decay_fwd, decay_bwd=l1_ret_decay_bwd,
                          gn_gain=l1_ret_gn, w_out=l1_w_out),
    ]
    layer_cache = [
        (cache_l0_attn_k, cache_l0_attn_v, cache_l0_swa_k, cache_l0_swa_v),
        (state_l1_ret_fwd, state_l1_ret_bwd),
    ]
    x_p, x_s = x_prompt, x_sample
    new_state = []
    for layer in range(DEPTH):
        mixer = layer_mixer[layer]
        x_p, st = _layer(x_p, c_ctx, *layer_common[layer], mixer=functools.partial(mixer, ctx=None))
        new_state.extend(st)
        x_s, _ = _layer(x_s, c, *layer_common[layer], mixer=functools.partial(mixer, ctx=layer_cache[layer]))
    y_prompt = _rms(x_p, final_norm)
    y_sample = _rms(x_s, final_norm)
    new_l0_attn_k, new_l0_attn_v, new_l0_swa_k, new_l0_swa_v, new_l1_ret_fwd, new_l1_ret_bwd = new_state
    return (y_prompt, y_sample, new_l0_attn_k, new_l0_attn_v, new_l0_swa_k, new_l0_swa_v, new_l1_ret_fwd, new_l1_ret_bwd)
```

```python
import functools

import jax
import jax.numpy as jnp
from jax import lax
from jax.experimental import pallas as pl
from jax.experimental.pallas import tpu as pltpu

F32 = jnp.float32
BF16 = jnp.bfloat16

D_MODEL = 1024
GRID_W = 64
HEAD_DIM = 64
N_Q_HEADS = 8
N_KV_HEADS = 2
GQA = N_Q_HEADS // N_KV_HEADS
WINDOW = 128
Q_BLOCK = 128
ROPE_THETA = 10000.0
RET_DK = 256
RET_DV = 512
RET_HEADS = 4
D_FF = 4 * D_MODEL
EPS = 1e-6
NEG_INF = -1e30
ADA_CHUNKS = 6

ATTN_Q_W = N_Q_HEADS * HEAD_DIM
ATTN_KV_W = N_KV_HEADS * HEAD_DIM
ATTN_IN_W = 2 * ATTN_Q_W + 4 * ATTN_KV_W
RET_QK_W = RET_HEADS * RET_DK
RET_V_W = RET_HEADS * RET_DV

VMEM_LIMIT_BYTES = 56 * 1024 * 1024
ROW_TILE = 256
ATTN_KEY_CHUNK = 512
FF_CHUNK = 1024
RET_CHUNK = 128


def _resident(shape):
    n = len(shape)
    return pl.BlockSpec(shape, lambda *_: (0,) * n, pipeline_mode=pl.Buffered(1))


def _params(n_axes):
    return pltpu.CompilerParams(
        dimension_semantics=("arbitrary",) * n_axes,
        vmem_limit_bytes=VMEM_LIMIT_BYTES)


def _silu(x):
    return x / (1.0 + jnp.exp(-x))


def _rms(x, gain):
    return x * lax.rsqrt(jnp.mean(x * x, axis=-1, keepdims=True) + EPS) * gain


def _modulated_norm(x, gain, scale, shift):
    return _rms(x, gain) * (1.0 + scale) + shift


def _swap_halves(x, s):
    w = x.shape[-1]
    lane = lax.broadcasted_iota(jnp.int32, x.shape, x.ndim - 1)
    fwd = pltpu.roll(x, w - s, x.ndim - 1)
    bwd = pltpu.roll(x, s, x.ndim - 1)
    return jnp.where((lane & s) == 0, fwd, bwd)


def _rope(x, cos, sin_signed, s):
    return x * cos + _swap_halves(x, s) * sin_signed


def _rope_tables(seq_len, head_dim, n_heads):
    rows = seq_len // GRID_W
    row = jnp.repeat(jnp.arange(rows, dtype=F32), GRID_W)
    col = jnp.tile(jnp.arange(GRID_W, dtype=F32), rows)
    n_freq = head_dim // 4
    inv = ROPE_THETA ** (-jnp.arange(n_freq, dtype=F32) / n_freq)
    ang_r = row[:, None] * inv[None, :]
    ang_c = col[:, None] * inv[None, :]
    cos = jnp.concatenate([jnp.cos(ang_r)] * 2 + [jnp.cos(ang_c)] * 2, axis=-1)
    sin = jnp.concatenate([-jnp.sin(ang_r), jnp.sin(ang_r), -jnp.sin(ang_c), jnp.sin(ang_c)], axis=-1)
    return jnp.tile(cos, (1, n_heads)), jnp.tile(sin, (1, n_heads))


def _ada_body(cond_ref, w_ref, b_ref, o_ref):
    a = _silu(cond_ref[...]).astype(BF16)
    o_ref[...] = jnp.dot(a, w_ref[...].astype(BF16), preferred_element_type=F32) + b_ref[...]


def _ada(cond, w, b):
    rows = cond.shape[0]
    n = w.shape[1]
    return pl.pallas_call(
        _ada_body,
        grid=(n // D_MODEL,),
        in_specs=[pl.BlockSpec((rows, D_MODEL), lambda j: (0, 0)),
                  pl.BlockSpec((D_MODEL, D_MODEL), lambda j: (0, j)),
                  pl.BlockSpec((1, D_MODEL), lambda j: (0, j))],
        out_specs=pl.BlockSpec((rows, D_MODEL), lambda j: (0, j)),
        out_shape=jax.ShapeDtypeStruct((rows, n), F32),
        compiler_params=_params(1),
        name="ada_modulation",
    )(cond, w, b.reshape(1, n))


def _head_rms(x, avg, gain):
    sq = x * x
    hi = sq.astype(BF16)
    lo = (sq - hi.astype(F32)).astype(BF16)
    ms = (jnp.dot(hi, avg, preferred_element_type=F32)
          + jnp.dot(lo, avg, preferred_element_type=F32))
    return x * lax.rsqrt(ms + EPS) * gain


def _pre0_body(*refs, rope, emit_state):
    it = iter(refs)
    x_ref, mod_ref, gain_ref, w_ref, avg_ref, qg_ref, kg_ref = (next(it) for _ in range(7))
    cos_ref = next(it) if rope else None
    sin_ref = next(it) if rope else None
    qkv_ref = next(it)
    state_ref = next(it) if emit_state else None

    h = _modulated_norm(x_ref[0], gain_ref[...], mod_ref[0, 1:2, :], mod_ref[0, 0:1, :]).astype(BF16)

    def proj(lo, width):
        return jnp.dot(h, w_ref[:, lo:lo + width], preferred_element_type=F32)

    qa = _head_rms(proj(0, ATTN_Q_W), avg_ref[...], qg_ref[...])
    ka = _head_rms(proj(ATTN_Q_W, ATTN_KV_W), avg_ref[:ATTN_KV_W, :ATTN_KV_W], kg_ref[...])
    va = proj(ATTN_Q_W + ATTN_KV_W, ATTN_KV_W)
    base = ATTN_Q_W + 2 * ATTN_KV_W
    qs = proj(base, ATTN_Q_W)
    ks = proj(base + ATTN_Q_W, ATTN_KV_W)
    vs = proj(base + ATTN_Q_W + ATTN_KV_W, ATTN_KV_W)
    if emit_state:
        state_ref[0] = jnp.concatenate([ka, va, ks, vs], axis=-1)
    if rope:
        cos, sin = cos_ref[...], sin_ref[...]
        s = HEAD_DIM // 4
        qa = _rope(qa, cos, sin, s)
        qs = _rope(qs, cos, sin, s)
        ka = _rope(ka, cos[:, :ATTN_KV_W], sin[:, :ATTN_KV_W], s)
        ks = _rope(ks, cos[:, :ATTN_KV_W], sin[:, :ATTN_KV_W], s)
    scale = HEAD_DIM ** -0.5
    qkv_ref[0] = jnp.concatenate(
        [qa * scale, ka, va, qs * scale, ks, vs], axis=-1).astype(BF16)


def _pre0(x, mod, gain, w_in, q_gain, k_gain, *, rope, emit_state):
    bsz, seq_len, _ = x.shape
    tm = min(ROW_TILE, seq_len)
    per_batch_mod = mod.shape[0] > 1
    head = lax.broadcasted_iota(jnp.int32, (ATTN_Q_W, ATTN_Q_W), 0) // HEAD_DIM
    avg = jnp.where(head == head.T, 1.0 / HEAD_DIM, 0.0).astype(BF16)
    inputs = [x, mod, gain.reshape(1, D_MODEL), w_in, avg,
              jnp.tile(q_gain, N_Q_HEADS).reshape(1, ATTN_Q_W),
              jnp.tile(k_gain, N_KV_HEADS).reshape(1, ATTN_KV_W)]
    in_specs = [pl.BlockSpec((1, tm, D_MODEL), lambda i, b: (b, i, 0)),
                pl.BlockSpec((1, ADA_CHUNKS, D_MODEL),
                             (lambda i, b: (b, 0, 0)) if per_batch_mod else (lambda i, b: (0, 0, 0))),
                _resident((1, D_MODEL)), _resident((D_MODEL, ATTN_IN_W)),
                _resident((ATTN_Q_W, ATTN_Q_W)), _resident((1, ATTN_Q_W)), _resident((1, ATTN_KV_W))]
    if rope:
        cos, sin = _rope_tables(seq_len, HEAD_DIM, N_Q_HEADS)
        inputs += [cos, sin]
        in_specs += [pl.BlockSpec((tm, ATTN_Q_W), lambda i, b: (i, 0))] * 2
    out_shape = [jax.ShapeDtypeStruct((bsz, seq_len, ATTN_IN_W), BF16)]
    out_specs = [pl.BlockSpec((1, tm, ATTN_IN_W), lambda i, b: (b, i, 0))]
    if emit_state:
        out_shape.append(jax.ShapeDtypeStruct((bsz, seq_len, 4 * ATTN_KV_W), F32))
        out_specs.append(pl.BlockSpec((1, tm, 4 * ATTN_KV_W), lambda i, b: (b, i, 0)))
    return pl.pallas_call(
        functools.partial(_pre0_body, rope=rope, emit_state=emit_state),
        grid=(seq_len // tm, bsz),
        in_specs=in_specs, out_specs=out_specs, out_shape=out_shape,
        compiler_params=_params(2),
        name="l0_in_proj",
    )(*inputs)


def _attn_body(*refs, seq_len, has_ctx, band, has_sink):
    it = iter(refs)
    sink_ref = next(it) if has_sink else None
    q_ref, k_ref, v_ref = next(it), next(it), next(it)
    ck_ref = next(it) if has_ctx else None
    cv_ref = next(it) if has_ctx else None
    o_ref = next(it)

    kvh = pl.program_id(1)
    qi = pl.program_id(2)
    tq = Q_BLOCK
    rows = GQA * tq

    q = q_ref[0]
    lane = lax.broadcasted_iota(jnp.int32, (tq, ATTN_KV_W), 1)
    mine = (lane // HEAD_DIM) == kvh
    zero = jnp.zeros((tq, ATTN_KV_W), BF16)
    pieces = []
    for g in range(GQA):
        piece = q[:, g * HEAD_DIM:(g + 1) * HEAD_DIM]
        pieces.append(jnp.where(mine, jnp.concatenate([piece, piece], axis=1), zero))
    qp = jnp.concatenate(pieces, axis=0)

    if has_sink:
        group = lax.broadcasted_iota(jnp.int32, (rows, 1), 0) // tq
        m = jnp.zeros((rows, 1), F32)
        for g in range(GQA):
            m = jnp.where(group == g, sink_ref[kvh * GQA + g], m)
        l = jnp.ones((rows, 1), F32)
    else:
        m = jnp.full((rows, 1), NEG_INF, F32)
        l = jnp.zeros((rows, 1), F32)
    acc = jnp.zeros((rows, ATTN_KV_W), F32)

    def update(carry, kc, vc, valid=None):
        m, l, acc = carry
        s = lax.dot_general(qp, kc, (((1,), (1,)), ((), ())), preferred_element_type=F32)
        if valid is not None:
            s = jnp.where(valid, s, NEG_INF)
        m_new = jnp.maximum(m, jnp.max(s, axis=-1, keepdims=True))
        alpha = jnp.exp(m - m_new)
        p = jnp.exp(s - m_new)
        l = alpha * l + jnp.sum(p, axis=-1, keepdims=True)
        acc = alpha * acc + jnp.dot(p.astype(BF16), vc, preferred_element_type=F32)
        return m_new, l, acc

    carry = (m, l, acc)
    if has_ctx:
        carry = update(carry, ck_ref[0], cv_ref[0])
    if band:
        width = Q_BLOCK + 2 * WINDOW
        start = pl.multiple_of(jnp.clip(qi * tq - WINDOW, 0, seq_len - width), Q_BLOCK)
        q_pos = qi * tq + lax.broadcasted_iota(jnp.int32, (rows, width), 0) % tq
        k_pos = start + lax.broadcasted_iota(jnp.int32, (rows, width), 1)
        valid = jnp.abs(q_pos - k_pos) <= WINDOW
        carry = update(carry, k_ref[0, pl.ds(start, width), :], v_ref[0, pl.ds(start, width), :], valid)
    else:
        tk = min(ATTN_KEY_CHUNK, seq_len)
        for c in range(seq_len // tk):
            carry = update(carry, k_ref[0, c * tk:(c + 1) * tk, :], v_ref[0, c * tk:(c + 1) * tk, :])
    m, l, acc = carry

    out = acc / l
    out = jnp.where(kvh == 0, out[:, :HEAD_DIM], out[:, HEAD_DIM:])
    o_ref[0] = jnp.concatenate(
        [out[g * tq:(g + 1) * tq, :] for g in range(GQA)], axis=1).astype(o_ref.dtype)


def _attention(qkv, q_col, k_col, v_col, *, ctx=None, sink=None, band=False):
    bsz, seq_len, _ = qkv.shape
    group_w = GQA * HEAD_DIM
    inputs, in_specs = [], []
    if sink is not None:
        inputs.append(sink)
        in_specs.append(pl.BlockSpec(memory_space=pltpu.SMEM))
    inputs += [qkv, qkv, qkv]
    in_specs += [pl.BlockSpec((1, Q_BLOCK, group_w), lambda b, h, i: (b, i, q_col + h)),
                 pl.BlockSpec((1, seq_len, ATTN_KV_W), lambda b, h, i: (b, 0, k_col)),
                 pl.BlockSpec((1, seq_len, ATTN_KV_W), lambda b, h, i: (b, 0, v_col))]
    if ctx is not None:
        past = ctx[0].shape[1]
        inputs += list(ctx)
        in_specs += [pl.BlockSpec((1, past, ATTN_KV_W), lambda b, h, i: (b, 0, 0))] * 2
    return pl.pallas_call(
        functools.partial(_attn_body, seq_len=seq_len, has_ctx=ctx is not None, band=band,
                          has_sink=sink is not None),
        grid=(bsz, N_KV_HEADS, seq_len // Q_BLOCK),
        in_specs=in_specs,
        out_specs=pl.BlockSpec((1, Q_BLOCK, group_w), lambda b, h, i: (b, i, h)),
        out_shape=jax.ShapeDtypeStruct((bsz, seq_len, ATTN_Q_W), BF16),
        compiler_params=_params(3),
        name="l0_attention_band" if band else "l0_attention_full",
    )(*inputs)


def _mlp_tail(x, y, mod_ref, nmlp_ref, w1_ref, w2_ref, fin_ref):
    x = x + mod_ref[0, 2:3, :] * y
    h = _modulated_norm(x, nmlp_ref[...], mod_ref[0, 4:5, :], mod_ref[0, 3:4, :]).astype(BF16)
    ff = jnp.zeros_like(x)
    for j in range(D_FF // FF_CHUNK):
        cols = slice(j * FF_CHUNK, (j + 1) * FF_CHUNK)
        hid = jnp.maximum(jnp.dot(h, w1_ref[:, cols], preferred_element_type=F32), 0.0)
        ff = ff + jnp.dot((hid * hid).astype(BF16), w2_ref[cols, :], preferred_element_type=F32)
    x = x + mod_ref[0, 5:6, :] * ff
    if fin_ref is not None:
        x = _rms(x, fin_ref[...])
    return x


def _post0_body(*refs, final):
    it = iter(refs)
    x_ref, oa_ref, os_ref, mod_ref, wo_ref, nmlp_ref, w1_ref, w2_ref = (next(it) for _ in range(8))
    fin_ref = next(it) if final else None
    o_ref = next(it)
    y = (jnp.dot(oa_ref[0], wo_ref[:ATTN_Q_W, :], preferred_element_type=F32)
         + jnp.dot(os_ref[0], wo_ref[ATTN_Q_W:, :], preferred_element_type=F32))
    o_ref[0] = _mlp_tail(x_ref[0], y, mod_ref, nmlp_ref, w1_ref, w2_ref, fin_ref)


def _post1_body(*refs, final):
    it = iter(refs)
    x_ref, o_in_ref, g_ref, gn_ref, mod_ref, wo_ref, nmlp_ref, w1_ref, w2_ref = (next(it) for _ in range(9))
    fin_ref = next(it) if final else None
    o_ref = next(it)
    y = jnp.zeros(x_ref.shape[1:], F32)
    for h in range(RET_HEADS):
        cols = slice(h * RET_DV, (h + 1) * RET_DV)
        normed = _rms(o_in_ref[0, :, cols], gn_ref[:, cols])
        gated = (_silu(g_ref[0, :, cols].astype(F32)) * normed).astype(BF16)
        y = y + jnp.dot(gated, wo_ref[cols, :], preferred_element_type=F32)
    o_ref[0] = _mlp_tail(x_ref[0], y, mod_ref, nmlp_ref, w1_ref, w2_ref, fin_ref)


def _post(body, x, mixer_inputs, mixer_specs, mod, w_out, norm_mlp, w1, w2, final_gain, name):
    bsz, seq_len, _ = x.shape
    tm = min(ROW_TILE, seq_len)
    per_batch_mod = mod.shape[0] > 1
    row_spec = pl.BlockSpec((1, tm, D_MODEL), lambda i, b: (b, i, 0))
    inputs = [x] + list(mixer_inputs) + [mod, w_out, norm_mlp.reshape(1, D_MODEL), w1, w2]
    in_specs = [row_spec] + list(mixer_specs(tm)) + [
        pl.BlockSpec((1, ADA_CHUNKS, D_MODEL),
                     (lambda i, b: (b, 0, 0)) if per_batch_mod else (lambda i, b: (0, 0, 0))),
        _resident(w_out.shape), _resident((1, D_MODEL)), _resident(w1.shape), _resident(w2.shape)]
    if final_gain is not None:
        inputs.append(final_gain.reshape(1, D_MODEL))
        in_specs.append(_resident((1, D_MODEL)))
    return pl.pallas_call(
        functools.partial(body, final=final_gain is not None),
        grid=(seq_len // tm, bsz),
        in_specs=in_specs, out_specs=row_spec,
        out_shape=jax.ShapeDtypeStruct(x.shape, F32),
        compiler_params=_params(2),
        name=name,
    )(*inputs)


def _post0(x, oa, os_, mod, w_out, norm_mlp, w1, w2):
    def specs(tm):
        return [pl.BlockSpec((1, tm, ATTN_Q_W), lambda i, b: (b, i, 0))] * 2
    return _post(_post0_body, x, [oa, os_], specs, mod, w_out, norm_mlp, w1, w2, None, "l0_out_proj_mlp")


def _post1(x, o_ret, g, gn_gain, mod, w_out, norm_mlp, w1, w2, final_gain):
    def specs(tm):
        return [pl.BlockSpec((1, tm, RET_V_W), lambda i, b: (b, i, 0)),
                pl.BlockSpec((1, tm, RET_V_W), lambda i, b: (b, i, 0)),
                _resident((1, RET_V_W))]
    return _post(_post1_body, x, [o_ret, g, gn_gain.reshape(1, RET_V_W)], specs, mod, w_out, norm_mlp,
                 w1, w2, final_gain, "l1_out_proj_mlp")


def _pre1_body(*refs, rope, chunk):
    it = iter(refs)
    x_ref, mod_ref, gain_ref, w_ref = (next(it) for _ in range(4))
    cos_ref = next(it) if rope else None
    sin_ref = next(it) if rope else None
    q_ref, kt_ref, v_ref, g_ref = (next(it) for _ in range(4))

    h = _modulated_norm(x_ref[0], gain_ref[...], mod_ref[0, 1:2, :], mod_ref[0, 0:1, :]).astype(BF16)

    def proj(lo, width):
        return jnp.dot(h, w_ref[:, lo:lo + width], preferred_element_type=F32)

    q = proj(0, RET_QK_W)
    k = proj(RET_QK_W, RET_QK_W)
    if rope:
        cos, sin = cos_ref[...], sin_ref[...]
        q = _rope(q, cos, sin, RET_DK // 4)
        k = _rope(k, cos, sin, RET_DK // 4)
    q_ref[0] = q.astype(BF16)
    k = k * (RET_DK ** -0.5)
    for j in range(k.shape[0] // chunk):
        kt_ref[0, j] = k[j * chunk:(j + 1) * chunk, :].T.astype(BF16)
    for j in range(RET_V_W // RET_QK_W):
        cols = slice(j * RET_QK_W, (j + 1) * RET_QK_W)
        v_ref[0, :, cols] = proj(2 * RET_QK_W + j * RET_QK_W, RET_QK_W).astype(BF16)
        g_ref[0, :, cols] = proj(2 * RET_QK_W + RET_V_W + j * RET_QK_W, RET_QK_W).astype(BF16)


def _pre1(x, mod, gain, w_in, *, rope, chunk):
    bsz, seq_len, _ = x.shape
    tm = min(ROW_TILE, seq_len)
    per_batch_mod = mod.shape[0] > 1
    inputs = [x, mod, gain.reshape(1, D_MODEL), w_in]
    in_specs = [pl.BlockSpec((1, tm, D_MODEL), lambda i, b: (b, i, 0)),
                pl.BlockSpec((1, ADA_CHUNKS, D_MODEL),
                             (lambda i, b: (b, 0, 0)) if per_batch_mod else (lambda i, b: (0, 0, 0))),
                _resident((1, D_MODEL)), _resident(w_in.shape)]
    if rope:
        cos, sin = _rope_tables(seq_len, RET_DK, RET_HEADS)
        inputs += [cos, sin]
        in_specs += [pl.BlockSpec((tm, RET_QK_W), lambda i, b: (i, 0))] * 2
    out_shape = [jax.ShapeDtypeStruct((bsz, seq_len, RET_QK_W), BF16),
                 jax.ShapeDtypeStruct((bsz, seq_len // chunk, RET_QK_W, chunk), BF16),
                 jax.ShapeDtypeStruct((bsz, seq_len, RET_V_W), BF16),
                 jax.ShapeDtypeStruct((bsz, seq_len, RET_V_W), BF16)]
    out_specs = [pl.BlockSpec((1, tm, RET_QK_W), lambda i, b: (b, i, 0)),
                 pl.BlockSpec((1, tm // chunk, RET_QK_W, chunk), lambda i, b: (b, i, 0, 0)),
                 pl.BlockSpec((1, tm, RET_V_W), lambda i, b: (b, i, 0)),
                 pl.BlockSpec((1, tm, RET_V_W), lambda i, b: (b, i, 0))]
    return pl.pallas_call(
        functools.partial(_pre1_body, rope=rope, chunk=chunk),
        grid=(seq_len // tm, bsz),
        in_specs=in_specs, out_specs=out_specs, out_shape=out_shape,
        compiler_params=_params(2),
        name="l1_in_proj",
    )(*inputs)


def _retention_body(*refs, n_chunks, chunk, has_init, emit_state):
    it = iter(refs)
    lgf_ref, lgb_ref, q_ref, kt_ref, v_ref = (next(it) for _ in range(5))
    s0f_ref = next(it) if has_init else None
    s0b_ref = next(it) if has_init else None
    o_ref = next(it)
    sf_out = next(it) if emit_state else None
    sb_out = next(it) if emit_state else None
    sf_ref, sb_ref = next(it), next(it)

    head = pl.program_id(1)
    lg_f = lgf_ref[head]
    lg_b = lgb_ref[head]
    c_f32 = float(chunk)

    i_idx = lax.broadcasted_iota(jnp.int32, (chunk, chunk), 0).astype(F32)
    j_idx = lax.broadcasted_iota(jnp.int32, (chunk, chunk), 1).astype(F32)
    diff = i_idx - j_idx
    decay = (jnp.exp(jnp.where(diff >= 0, diff * lg_f, NEG_INF))
             + jnp.exp(jnp.where(diff <= 0, -diff * lg_b, NEG_INF)))
    row = lax.broadcasted_iota(jnp.int32, (chunk, 1), 0).astype(F32)
    lanes = lax.broadcasted_iota(jnp.int32, (1, chunk), 1).astype(F32)
    q_decay_f = jnp.exp((row + 1.0) * lg_f)
    q_decay_b = jnp.exp((c_f32 - row) * lg_b)
    k_decay_f = jnp.exp((c_f32 - 1.0 - lanes) * lg_f)
    k_decay_b = jnp.exp(lanes * lg_b)
    one = jnp.ones((1, 1), F32)
    chunk_decay_f = jnp.exp(one * (c_f32 * lg_f))
    chunk_decay_b = jnp.exp(one * (c_f32 * lg_b))

    if has_init:
        sf_ref[...] = s0f_ref[0, 0]
        sb_ref[...] = s0b_ref[0, 0]
    else:
        sf_ref[...] = jnp.zeros_like(sf_ref)
        sb_ref[...] = jnp.zeros_like(sb_ref)

    def rows_of(c):
        return pl.ds(pl.multiple_of(c * chunk, chunk), chunk)

    def forward(c, _):
        qc = q_ref[0, rows_of(c), :]
        ktc = kt_ref[0, c]
        vc = v_ref[0, rows_of(c), :]
        att = jnp.dot(qc, ktc, preferred_element_type=F32) * decay
        intra = jnp.dot(att.astype(BF16), vc, preferred_element_type=F32)
        inter = jnp.dot(qc, sf_ref[...].astype(BF16), preferred_element_type=F32) * q_decay_f
        o_ref[0, rows_of(c), :] = intra + inter
        sf_ref[...] = chunk_decay_f * sf_ref[...] + jnp.dot(
            (ktc.astype(F32) * k_decay_f).astype(BF16), vc, preferred_element_type=F32)
        return 0

    def backward(t, _):
        c = n_chunks - 1 - t
        qc = q_ref[0, rows_of(c), :]
        ktc = kt_ref[0, c]
        vc = v_ref[0, rows_of(c), :]
        inter = jnp.dot(qc, sb_ref[...].astype(BF16), preferred_element_type=F32) * q_decay_b
        o_ref[0, rows_of(c), :] = o_ref[0, rows_of(c), :] + inter
        sb_ref[...] = chunk_decay_b * sb_ref[...] + jnp.dot(
            (ktc.astype(F32) * k_decay_b).astype(BF16), vc, preferred_element_type=F32)
        return 0

    lax.fori_loop(0, n_chunks, forward, 0)
    lax.fori_loop(0, n_chunks, backward, 0)
    if emit_state:
        sf_out[0, 0] = sf_ref[...]
        sb_out[0, 0] = sb_ref[...]


def _retention(q, kt, v, lg_f, lg_b, init, *, chunk, emit_state):
    bsz, seq_len, _ = q.shape
    n_chunks = seq_len // chunk
    smem = pl.BlockSpec(memory_space=pltpu.SMEM)
    state_spec = pl.BlockSpec((1, 1, RET_DK, RET_DV), lambda b, h: (b, h, 0, 0))
    inputs = [lg_f, lg_b, q, kt, v]
    in_specs = [smem, smem,
                pl.BlockSpec((1, seq_len, RET_DK), lambda b, h: (b, 0, h)),
                pl.BlockSpec((1, n_chunks, RET_DK, chunk), lambda b, h: (b, 0, h, 0)),
                pl.BlockSpec((1, seq_len, RET_DV), lambda b, h: (b, 0, h))]
    if init is not None:
        inputs += list(init)
        in_specs += [state_spec, state_spec]
    out_shape = [jax.ShapeDtypeStruct((bsz, seq_len, RET_V_W), F32)]
    out_specs = [pl.BlockSpec((1, seq_len, RET_DV), lambda b, h: (b, 0, h))]
    if emit_state:
        out_shape += [jax.ShapeDtypeStruct((bsz, RET_HEADS, RET_DK, RET_DV), F32)] * 2
        out_specs += [state_spec, state_spec]
    return pl.pallas_call(
        functools.partial(_retention_body, n_chunks=n_chunks, chunk=chunk,
                          has_init=init is not None, emit_state=emit_state),
        grid=(bsz, RET_HEADS),
        in_specs=in_specs, out_specs=out_specs, out_shape=out_shape,
        scratch_shapes=[pltpu.VMEM((RET_DK, RET_DV), F32), pltpu.VMEM((RET_DK, RET_DV), F32)],
        compiler_params=_params(2),
        name="l1_retention",
    )(*inputs)


def _layer0(x, mod, norm_mix, norm_mlp, w_in, q_gain, k_gain, sink, w_out, w1, w2, ctx):
    sample = ctx is not None
    outs = _pre0(x, mod, norm_mix, w_in, q_gain, k_gain, rope=sample, emit_state=not sample)
    qkv = outs[0]
    ctx_a = ctx_s = None
    if sample:
        bsz, past = ctx[0].shape[:2]
        ctx_a = [t.reshape(bsz, past, ATTN_KV_W).astype(BF16) for t in ctx[:2]]
        ctx_s = [t.reshape(bsz, past, ATTN_KV_W).astype(BF16) for t in ctx[2:]]
    out_a = _attention(qkv, 0, 4, 5, ctx=ctx_a)
    out_s = _attention(qkv, 3, 10, 11, ctx=ctx_s, sink=sink, band=sample)
    x = _post0(x, out_a, out_s, mod, w_out, norm_mlp, w1, w2)
    state = ()
    if not sample:
        bsz, seq_len, _ = x.shape
        st = outs[1].reshape(bsz, seq_len, 4, N_KV_HEADS, HEAD_DIM)
        state = tuple(st[:, :, j] for j in range(4))
    return x, state


def _layer1(x, mod, norm_mix, norm_mlp, w_in, lg_f, lg_b, gn_gain, w_out, w1, w2, final_gain, ctx):
    sample = ctx is not None
    q, kt, v, g = _pre1(x, mod, norm_mix, w_in, rope=sample, chunk=RET_CHUNK)
    outs = _retention(q, kt, v, lg_f, lg_b, ctx, chunk=RET_CHUNK, emit_state=not sample)
    x = _post1(x, outs[0], g, gn_gain, mod, w_out, norm_mlp, w1, w2, final_gain)
    return x, tuple(outs[1:])


def kernel(x_prompt, x_sample, c, cache_l0_attn_k, cache_l0_attn_v, cache_l0_swa_k, cache_l0_swa_v, state_l1_ret_fwd, state_l1_ret_bwd, c_ctx, l0_ada_w, l0_ada_b, l0_norm_mix, l0_norm_mlp, l0_w_in, l0_q_norm, l0_k_norm, l0_sink, l0_w_out, l0_mlp_w1, l0_mlp_w2, l1_ada_w, l1_ada_b, l1_norm_mix, l1_norm_mlp, l1_w_in, l1_ret_decay_fwd, l1_ret_decay_bwd, l1_ret_gn, l1_w_out, l1_mlp_w1, l1_mlp_w2, final_norm):
    n_dec = c.shape[0]
    pad = (-(n_dec + 1)) % 8
    cond = jnp.concatenate([c, c_ctx[None, :], jnp.zeros((pad, D_MODEL), F32)], axis=0)

    def modulation(w, b):
        mod = _ada(cond, w, b).reshape(cond.shape[0], ADA_CHUNKS, D_MODEL)
        return mod[n_dec:n_dec + 1], mod[:n_dec]

    mod0_p, mod0_s = modulation(l0_ada_w, l0_ada_b)
    mod1_p, mod1_s = modulation(l1_ada_w, l1_ada_b)

    l0 = (l0_norm_mix, l0_norm_mlp, l0_w_in.astype(BF16), l0_q_norm, l0_k_norm, l0_sink,
          l0_w_out.astype(BF16), l0_mlp_w1.astype(BF16), l0_mlp_w2.astype(BF16))
    x_p, st0 = _layer0(x_prompt, mod0_p, *l0, ctx=None)
    x_s, _ = _layer0(x_sample, mod0_s, *l0,
                     ctx=(cache_l0_attn_k, cache_l0_attn_v, cache_l0_swa_k, cache_l0_swa_v))

    l1 = (l1_norm_mix, l1_norm_mlp, l1_w_in.astype(BF16),
          jax.nn.log_sigmoid(l1_ret_decay_fwd.astype(F32)), jax.nn.log_sigmoid(l1_ret_decay_bwd.astype(F32)),
          l1_ret_gn, l1_w_out.astype(BF16), l1_mlp_w1.astype(BF16), l1_mlp_w2.astype(BF16), final_norm)
    y_p, st1 = _layer1(x_p, mod1_p, *l1, ctx=None)
    y_s, _ = _layer1(x_s, mod1_s, *l1, ctx=(state_l1_ret_fwd, state_l1_ret_bwd))
    return (y_p, y_s) + st0 + st1
```

```python
import functools

import jax
import jax.numpy as jnp
from jax import lax
from jax.experimental import pallas as pl
from jax.experimental.pallas import tpu as pltpu

F32 = jnp.float32
BF16 = jnp.bfloat16

D_MODEL = 1024
GRID_W = 64
HEAD_DIM = 64
N_Q_HEADS = 8
N_KV_HEADS = 2
GQA = N_Q_HEADS // N_KV_HEADS
WINDOW = 128
Q_BLOCK = 128
ROPE_THETA = 10000.0
RET_DK = 256
RET_DV = 512
RET_HEADS = 4
D_FF = 4 * D_MODEL
EPS = 1e-6
NEG_INF = -1e30
ADA_CHUNKS = 6

ATTN_Q_W = N_Q_HEADS * HEAD_DIM
ATTN_KV_W = N_KV_HEADS * HEAD_DIM
ATTN_IN_W = 2 * ATTN_Q_W + 4 * ATTN_KV_W
QK_W = 2 * ATTN_Q_W + 2 * ATTN_KV_W
LOG2_E = 1.4426950408889634
Q_SCALE = HEAD_DIM ** -0.5 * LOG2_E
DENOM_ROWS = 16
RET_QK_W = RET_HEADS * RET_DK
RET_V_W = RET_HEADS * RET_DV

VMEM_LIMIT_BYTES = 56 * 1024 * 1024
ROW_TILE = 256
ATTN_KEY_CHUNK = 512
KEY_SUB = 256
FF_CHUNK = 1024
RET_CHUNK = 256


def _resident(shape):
    n = len(shape)
    return pl.BlockSpec(shape, lambda *_: (0,) * n, pipeline_mode=pl.Buffered(1))


def _params(n_axes):
    return pltpu.CompilerParams(
        dimension_semantics=("arbitrary",) * n_axes,
        vmem_limit_bytes=VMEM_LIMIT_BYTES)


def _silu(x):
    return x / (1.0 + jnp.exp(-x))


def _rms(x, gain):
    return x * lax.rsqrt(jnp.mean(x * x, axis=-1, keepdims=True) + EPS) * gain


def _modulated_norm(x, gain, scale, shift):
    return _rms(x, gain) * (1.0 + scale) + shift


def _swap_halves(x, s):
    w = x.shape[-1]
    lane = lax.broadcasted_iota(jnp.int32, x.shape, x.ndim - 1)
    fwd = pltpu.roll(x, w - s, x.ndim - 1)
    bwd = pltpu.roll(x, s, x.ndim - 1)
    return jnp.where((lane & s) == 0, fwd, bwd)


def _rope(x, cos, sin_signed, s):
    return x * cos + _swap_halves(x, s) * sin_signed


def _rope_tables(seq_len, head_dim, n_heads):
    rows = seq_len // GRID_W
    row = jnp.repeat(jnp.arange(rows, dtype=F32), GRID_W)
    col = jnp.tile(jnp.arange(GRID_W, dtype=F32), rows)
    n_freq = head_dim // 4
    inv = ROPE_THETA ** (-jnp.arange(n_freq, dtype=F32) / n_freq)
    ang_r = row[:, None] * inv[None, :]
    ang_c = col[:, None] * inv[None, :]
    cos = jnp.concatenate([jnp.cos(ang_r)] * 2 + [jnp.cos(ang_c)] * 2, axis=-1)
    sin = jnp.concatenate([-jnp.sin(ang_r), jnp.sin(ang_r), -jnp.sin(ang_c), jnp.sin(ang_c)], axis=-1)
    return jnp.tile(cos, (1, n_heads)), jnp.tile(sin, (1, n_heads))


def _ada_body(cond_ref, w_ref, b_ref, o_ref):
    a = _silu(cond_ref[...]).astype(BF16)
    o_ref[...] = jnp.dot(a, w_ref[...].astype(BF16), preferred_element_type=F32) + b_ref[...]


def _ada(cond, w, b):
    rows = cond.shape[0]
    n = w.shape[1]
    return pl.pallas_call(
        _ada_body,
        grid=(n // D_MODEL,),
        in_specs=[pl.BlockSpec((rows, D_MODEL), lambda j: (0, 0)),
                  pl.BlockSpec((D_MODEL, D_MODEL), lambda j: (0, j)),
                  pl.BlockSpec((1, D_MODEL), lambda j: (0, j))],
        out_specs=pl.BlockSpec((rows, D_MODEL), lambda j: (0, j)),
        out_shape=jax.ShapeDtypeStruct((rows, n), F32),
        compiler_params=_params(1),
        name="ada_modulation",
    )(cond, w, b.reshape(1, n))


def _head_rms(x, avg, gain):
    sq = x * x
    hi = sq.astype(BF16)
    lo = (sq - hi.astype(F32)).astype(BF16)
    ms = (jnp.dot(hi, avg, preferred_element_type=F32)
          + jnp.dot(lo, avg, preferred_element_type=F32))
    return x * lax.rsqrt(ms + EPS) * gain


def _pre0_body(*refs, rope, emit_state):
    it = iter(refs)
    x_ref, mod_ref, gain_ref, w_ref, avg_ref, qg_ref, kg_ref = (next(it) for _ in range(7))
    cos_ref = next(it) if rope else None
    sin_ref = next(it) if rope else None
    qk_ref, vt_ref = next(it), next(it)
    state_ref = next(it) if emit_state else None

    h = _modulated_norm(x_ref[0], gain_ref[...], mod_ref[0, 1:2, :], mod_ref[0, 0:1, :]).astype(BF16)

    def proj(lo, width):
        return jnp.dot(h, w_ref[:, lo:lo + width], preferred_element_type=F32)

    qa = _head_rms(proj(0, ATTN_Q_W), avg_ref[...], qg_ref[...])
    ka = _head_rms(proj(ATTN_Q_W, ATTN_KV_W), avg_ref[:ATTN_KV_W, :ATTN_KV_W], kg_ref[...])
    va = proj(ATTN_Q_W + ATTN_KV_W, ATTN_KV_W)
    base = ATTN_Q_W + 2 * ATTN_KV_W
    qs = proj(base, ATTN_Q_W)
    ks = proj(base + ATTN_Q_W, ATTN_KV_W)
    vs = proj(base + ATTN_Q_W + ATTN_KV_W, ATTN_KV_W)
    if emit_state:
        state_ref[0] = jnp.concatenate([ka, va, ks, vs], axis=-1)
    if rope:
        cos, sin = cos_ref[...], sin_ref[...]
        s = HEAD_DIM // 4
        qa = _rope(qa, cos, sin, s)
        qs = _rope(qs, cos, sin, s)
        ka = _rope(ka, cos[:, :ATTN_KV_W], sin[:, :ATTN_KV_W], s)
        ks = _rope(ks, cos[:, :ATTN_KV_W], sin[:, :ATTN_KV_W], s)
    qk_ref[0] = jnp.concatenate(
        [qa * Q_SCALE, qs * Q_SCALE, ka, ks], axis=-1).astype(BF16)
    vat = va.T.astype(BF16)
    vst = vs.T.astype(BF16)
    for j in range(N_KV_HEADS):
        vt_ref[0, j] = vat[j * HEAD_DIM:(j + 1) * HEAD_DIM, :]
        vt_ref[0, N_KV_HEADS + j] = vst[j * HEAD_DIM:(j + 1) * HEAD_DIM, :]


def _pre0(x, mod, gain, w_in, q_gain, k_gain, *, rope, emit_state):
    bsz, seq_len, _ = x.shape
    tm = min(ROW_TILE, seq_len)
    per_batch_mod = mod.shape[0] > 1
    head = lax.broadcasted_iota(jnp.int32, (ATTN_Q_W, ATTN_Q_W), 0) // HEAD_DIM
    avg = jnp.where(head == head.T, 1.0 / HEAD_DIM, 0.0).astype(BF16)
    inputs = [x, mod, gain.reshape(1, D_MODEL), w_in, avg,
              jnp.tile(q_gain, N_Q_HEADS).reshape(1, ATTN_Q_W),
              jnp.tile(k_gain, N_KV_HEADS).reshape(1, ATTN_KV_W)]
    in_specs = [pl.BlockSpec((1, tm, D_MODEL), lambda i, b: (b, i, 0)),
                pl.BlockSpec((1, ADA_CHUNKS, D_MODEL),
                             (lambda i, b: (b, 0, 0)) if per_batch_mod else (lambda i, b: (0, 0, 0))),
                _resident((1, D_MODEL)), _resident((D_MODEL, ATTN_IN_W)),
                _resident((ATTN_Q_W, ATTN_Q_W)), _resident((1, ATTN_Q_W)), _resident((1, ATTN_KV_W))]
    if rope:
        cos, sin = _rope_tables(seq_len, HEAD_DIM, N_Q_HEADS)
        inputs += [cos, sin]
        in_specs += [pl.BlockSpec((tm, ATTN_Q_W), lambda i, b: (i, 0))] * 2
    out_shape = [jax.ShapeDtypeStruct((bsz, seq_len, QK_W), BF16),
                 jax.ShapeDtypeStruct((bsz, 2 * N_KV_HEADS, HEAD_DIM, seq_len), BF16)]
    out_specs = [pl.BlockSpec((1, tm, QK_W), lambda i, b: (b, i, 0)),
                 pl.BlockSpec((1, 2 * N_KV_HEADS, HEAD_DIM, tm), lambda i, b: (b, 0, 0, i))]
    if emit_state:
        out_shape.append(jax.ShapeDtypeStruct((bsz, seq_len, 4 * ATTN_KV_W), F32))
        out_specs.append(pl.BlockSpec((1, tm, 4 * ATTN_KV_W), lambda i, b: (b, i, 0)))
    return pl.pallas_call(
        functools.partial(_pre0_body, rope=rope, emit_state=emit_state),
        grid=(seq_len // tm, bsz),
        in_specs=in_specs, out_specs=out_specs, out_shape=out_shape,
        compiler_params=_params(2),
        name="l0_in_proj",
    )(*inputs)


def _attn_body(*refs, seq_len, has_ctx, band, has_sink):
    it = iter(refs)
    sink_ref = next(it) if has_sink else None
    q_ref, k_ref, vt_ref = next(it), next(it), next(it)
    ck_ref = next(it) if has_ctx else None
    cvt_ref = next(it) if has_ctx else None
    bias_ref = next(it) if band else None
    o_ref = next(it)

    qi = pl.program_id(1)
    tq = Q_BLOCK
    cols = GQA * tq
    heads = range(N_KV_HEADS)

    qt = q_ref[0].astype(F32).T
    qpt = []
    for h in heads:
        rows = [qt[(h * GQA + g) * HEAD_DIM:(h * GQA + g + 1) * HEAD_DIM, :] for g in range(GQA)]
        qh = jnp.concatenate(rows, axis=1)
        halves = [jnp.zeros_like(qh)] * N_KV_HEADS
        halves[h] = qh
        qpt.append(jnp.concatenate(halves, axis=0).astype(BF16))

    m, acc = [], []
    for h in heads:
        if has_sink:
            group = lax.broadcasted_iota(jnp.int32, (1, cols), 1) // tq
            m_h = jnp.zeros((1, cols), F32)
            for g in range(GQA):
                m_h = jnp.where(group == g, sink_ref[h * GQA + g] * LOG2_E, m_h)
            denom0 = 1.0
        else:
            m_h = jnp.full((1, cols), NEG_INF, F32)
            denom0 = 0.0
        m.append(m_h)
        acc.append(jnp.concatenate([jnp.zeros((HEAD_DIM, cols), F32),
                                    jnp.full((DENOM_ROWS, cols), denom0, F32)], axis=0))

    def static_subs(k_at, vt_at, lo, n_keys):
        return [(KEY_SUB, functools.partial(k_at, lo + o), functools.partial(vt_at, lo + o), None)
                for o in range(0, n_keys, KEY_SUB)]

    chunks = []
    if has_ctx:
        chunks.append(static_subs(lambda o: ck_ref[0, o:o + KEY_SUB, :],
                                  lambda o, h: cvt_ref[0, h, :, o:o + KEY_SUB], 0, ck_ref.shape[1]))
    if band:
        width = Q_BLOCK + 2 * WINDOW
        start = jnp.clip(qi * tq - WINDOW, 0, seq_len - width)
        variant = jnp.where(qi == 0, 0, jnp.where(qi == seq_len // tq - 1, 2, 1))

        def window_rows(o):
            return pl.ds(pl.multiple_of(start + o, Q_BLOCK), Q_BLOCK)

        chunks.append([(Q_BLOCK,
                        functools.partial(lambda o: k_ref[0, window_rows(o), :], o),
                        functools.partial(lambda o, h: vt_ref[0, h, :, window_rows(o)], o),
                        functools.partial(lambda o: bias_ref[variant, o:o + Q_BLOCK, :], o))
                       for o in range(0, width, Q_BLOCK)])
    else:
        tk = min(ATTN_KEY_CHUNK, seq_len)
        for c in range(seq_len // tk):
            chunks.append(static_subs(lambda o: k_ref[0, o:o + KEY_SUB, :],
                                      lambda o, h: vt_ref[0, h, :, o:o + KEY_SUB], c * tk, tk))

    def scores(sub, h):
        _, keys, _, bias = sub
        st = jnp.dot(keys(), qpt[h], preferred_element_type=F32)
        return st if bias is None else st + bias()

    def col_max(a, b):
        b = jnp.max(b, axis=0, keepdims=True)
        return b if a is None else jnp.maximum(a, b)

    def rescale(h, chunk_max):
        m_new = jnp.maximum(m[h], chunk_max)
        acc[h] = acc[h] * jnp.exp2(m[h] - m_new)
        m[h] = m_new

    cur = [[] for _ in heads]
    chunk_max = [None for _ in heads]
    for sub in chunks[0]:
        for h in heads:
            cur[h].append(scores(sub, h))
            chunk_max[h] = col_max(chunk_max[h], cur[h][-1])
    for h in heads:
        rescale(h, chunk_max[h])
    for c, chunk in enumerate(chunks):
        nxt_chunk = chunks[c + 1] if c + 1 < len(chunks) else []
        nxt = [[] for _ in heads]
        chunk_max = [None for _ in heads]
        for s in range(max(len(chunk), len(nxt_chunk))):
            for h in heads:
                if s < len(nxt_chunk):
                    nxt[h].append(scores(nxt_chunk[s], h))
                    chunk_max[h] = col_max(chunk_max[h], nxt[h][-1])
                if s < len(chunk):
                    n_keys, _, values_t, _ = chunk[s]
                    pt = jnp.exp2(cur[h][s] - m[h]).astype(BF16)
                    lhs = jnp.concatenate([values_t(h), jnp.ones((DENOM_ROWS, n_keys), BF16)], axis=0)
                    acc[h] = acc[h] + jnp.dot(lhs, pt, preferred_element_type=F32)
        if nxt_chunk:
            for h in heads:
                rescale(h, chunk_max[h])
            cur = nxt

    pieces = []
    for h in heads:
        out_t = acc[h][:HEAD_DIM, :] / acc[h][HEAD_DIM:HEAD_DIM + 1, :]
        pieces += [out_t[:, g * tq:(g + 1) * tq] for g in range(GQA)]
    o_ref[0] = jnp.concatenate(pieces, axis=0).T.astype(o_ref.dtype)


def _band_bias():
    width = Q_BLOCK + 2 * WINDOW
    r = lax.broadcasted_iota(jnp.int32, (width, GQA * Q_BLOCK), 0)
    t = lax.broadcasted_iota(jnp.int32, (width, GQA * Q_BLOCK), 1) % Q_BLOCK
    rel = jnp.stack([r - t, r - t - WINDOW, r - t - 2 * WINDOW])
    return jnp.where(jnp.abs(rel) <= WINDOW, 0.0, NEG_INF).astype(F32)


def _attention(qk, vt, group, *, ctx=None, sink=None, band=False):
    bsz, seq_len, _ = qk.shape
    k_col = 2 * ATTN_Q_W // ATTN_KV_W + group
    inputs, in_specs = [], []
    if sink is not None:
        inputs.append(sink)
        in_specs.append(pl.BlockSpec(memory_space=pltpu.SMEM))
    inputs += [qk, qk, vt]
    in_specs += [pl.BlockSpec((1, Q_BLOCK, ATTN_Q_W), lambda b, i: (b, i, group)),
                 pl.BlockSpec((1, seq_len, ATTN_KV_W), lambda b, i: (b, 0, k_col)),
                 pl.BlockSpec((1, N_KV_HEADS, HEAD_DIM, seq_len), lambda b, i: (b, group, 0, 0))]
    if ctx is not None:
        past = ctx[0].shape[1]
        inputs += list(ctx)
        in_specs += [pl.BlockSpec((1, past, ATTN_KV_W), lambda b, i: (b, 0, 0)),
                     pl.BlockSpec((1, N_KV_HEADS, HEAD_DIM, past), lambda b, i: (b, 0, 0, 0))]
    if band:
        assert seq_len >= Q_BLOCK + 2 * WINDOW
        bias = _band_bias()
        inputs.append(bias)
        in_specs.append(_resident(bias.shape))
    return pl.pallas_call(
        functools.partial(_attn_body, seq_len=seq_len, has_ctx=ctx is not None, band=band,
                          has_sink=sink is not None),
        grid=(bsz, seq_len // Q_BLOCK),
        in_specs=in_specs,
        out_specs=pl.BlockSpec((1, Q_BLOCK, ATTN_Q_W), lambda b, i: (b, i, 0)),
        out_shape=jax.ShapeDtypeStruct((bsz, seq_len, ATTN_Q_W), BF16),
        compiler_params=_params(2),
        name="l0_attention_band" if band else "l0_attention_full",
    )(*inputs)


def _mlp_tail(x, y, mod_ref, nmlp_ref, w1_ref, w2_ref, fin_ref):
    x = x + mod_ref[0, 2:3, :] * y
    h = _modulated_norm(x, nmlp_ref[...], mod_ref[0, 4:5, :], mod_ref[0, 3:4, :]).astype(BF16)
    ff = jnp.zeros_like(x)
    for j in range(D_FF // FF_CHUNK):
        cols = slice(j * FF_CHUNK, (j + 1) * FF_CHUNK)
        hid = jnp.maximum(jnp.dot(h, w1_ref[:, cols], preferred_element_type=F32), 0.0)
        ff = ff + jnp.dot((hid * hid).astype(BF16), w2_ref[cols, :], preferred_element_type=F32)
    x = x + mod_ref[0, 5:6, :] * ff
    if fin_ref is not None:
        x = _rms(x, fin_ref[...])
    return x


def _post0_body(*refs, final):
    it = iter(refs)
    x_ref, oa_ref, os_ref, mod_ref, wo_ref, nmlp_ref, w1_ref, w2_ref = (next(it) for _ in range(8))
    fin_ref = next(it) if final else None
    o_ref = next(it)
    y = (jnp.dot(oa_ref[0], wo_ref[:ATTN_Q_W, :], preferred_element_type=F32)
         + jnp.dot(os_ref[0], wo_ref[ATTN_Q_W:, :], preferred_element_type=F32))
    o_ref[0] = _mlp_tail(x_ref[0], y, mod_ref, nmlp_ref, w1_ref, w2_ref, fin_ref)


def _post1_body(*refs, final):
    it = iter(refs)
    x_ref, o_in_ref, g_ref, gn_ref, mod_ref, wo_ref, nmlp_ref, w1_ref, w2_ref = (next(it) for _ in range(9))
    fin_ref = next(it) if final else None
    o_ref = next(it)
    y = jnp.zeros(x_ref.shape[1:], F32)
    for h in range(RET_HEADS):
        cols = slice(h * RET_DV, (h + 1) * RET_DV)
        normed = _rms(o_in_ref[0, :, cols], gn_ref[:, cols])
        gated = (_silu(g_ref[0, :, cols].astype(F32)) * normed).astype(BF16)
        y = y + jnp.dot(gated, wo_ref[cols, :], preferred_element_type=F32)
    o_ref[0] = _mlp_tail(x_ref[0], y, mod_ref, nmlp_ref, w1_ref, w2_ref, fin_ref)


def _post(body, x, mixer_inputs, mixer_specs, mod, w_out, norm_mlp, w1, w2, final_gain, name):
    bsz, seq_len, _ = x.shape
    tm = min(ROW_TILE, seq_len)
    per_batch_mod = mod.shape[0] > 1
    row_spec = pl.BlockSpec((1, tm, D_MODEL), lambda i, b: (b, i, 0))
    inputs = [x] + list(mixer_inputs) + [mod, w_out, norm_mlp.reshape(1, D_MODEL), w1, w2]
    in_specs = [row_spec] + list(mixer_specs(tm)) + [
        pl.BlockSpec((1, ADA_CHUNKS, D_MODEL),
                     (lambda i, b: (b, 0, 0)) if per_batch_mod else (lambda i, b: (0, 0, 0))),
        _resident(w_out.shape), _resident((1, D_MODEL)), _resident(w1.shape), _resident(w2.shape)]
    if final_gain is not None:
        inputs.append(final_gain.reshape(1, D_MODEL))
        in_specs.append(_resident((1, D_MODEL)))
    return pl.pallas_call(
        functools.partial(body, final=final_gain is not None),
        grid=(seq_len // tm, bsz),
        in_specs=in_specs, out_specs=row_spec,
        out_shape=jax.ShapeDtypeStruct(x.shape, F32),
        compiler_params=_params(2),
        name=name,
    )(*inputs)


def _post0(x, oa, os_, mod, w_out, norm_mlp, w1, w2):
    def specs(tm):
        return [pl.BlockSpec((1, tm, ATTN_Q_W), lambda i, b: (b, i, 0))] * 2
    return _post(_post0_body, x, [oa, os_], specs, mod, w_out, norm_mlp, w1, w2, None, "l0_out_proj_mlp")


def _post1(x, o_ret, g, gn_gain, mod, w_out, norm_mlp, w1, w2, final_gain):
    def specs(tm):
        return [pl.BlockSpec((1, tm, RET_V_W), lambda i, b: (b, i, 0)),
                pl.BlockSpec((1, tm, RET_V_W), lambda i, b: (b, i, 0)),
                _resident((1, RET_V_W))]
    return _post(_post1_body, x, [o_ret, g, gn_gain.reshape(1, RET_V_W)], specs, mod, w_out, norm_mlp,
                 w1, w2, final_gain, "l1_out_proj_mlp")


def _pre1_body(*refs, rope, chunk):
    it = iter(refs)
    x_ref, mod_ref, gain_ref, w_ref = (next(it) for _ in range(4))
    cos_ref = next(it) if rope else None
    sin_ref = next(it) if rope else None
    q_ref, kt_ref, v_ref, g_ref = (next(it) for _ in range(4))

    h = _modulated_norm(x_ref[0], gain_ref[...], mod_ref[0, 1:2, :], mod_ref[0, 0:1, :]).astype(BF16)

    def proj(lo, width):
        return jnp.dot(h, w_ref[:, lo:lo + width], preferred_element_type=F32)

    q = proj(0, RET_QK_W)
    k = proj(RET_QK_W, RET_QK_W)
    if rope:
        cos, sin = cos_ref[...], sin_ref[...]
        q = _rope(q, cos, sin, RET_DK // 4)
        k = _rope(k, cos, sin, RET_DK // 4)
    q_ref[0] = q.astype(BF16)
    k = k * (RET_DK ** -0.5)
    for j in range(k.shape[0] // chunk):
        kt_ref[0, j] = k[j * chunk:(j + 1) * chunk, :].T.astype(BF16)
    for j in range(RET_V_W // RET_QK_W):
        cols = slice(j * RET_QK_W, (j + 1) * RET_QK_W)
        v_ref[0, :, cols] = proj(2 * RET_QK_W + j * RET_QK_W, RET_QK_W).astype(BF16)
        g_ref[0, :, cols] = proj(2 * RET_QK_W + RET_V_W + j * RET_QK_W, RET_QK_W).astype(BF16)


def _pre1(x, mod, gain, w_in, *, rope, chunk):
    bsz, seq_len, _ = x.shape
    tm = min(ROW_TILE, seq_len)
    per_batch_mod = mod.shape[0] > 1
    inputs = [x, mod, gain.reshape(1, D_MODEL), w_in]
    in_specs = [pl.BlockSpec((1, tm, D_MODEL), lambda i, b: (b, i, 0)),
                pl.BlockSpec((1, ADA_CHUNKS, D_MODEL),
                             (lambda i, b: (b, 0, 0)) if per_batch_mod else (lambda i, b: (0, 0, 0))),
                _resident((1, D_MODEL)), _resident(w_in.shape)]
    if rope:
        cos, sin = _rope_tables(seq_len, RET_DK, RET_HEADS)
        inputs += [cos, sin]
        in_specs += [pl.BlockSpec((tm, RET_QK_W), lambda i, b: (i, 0))] * 2
    out_shape = [jax.ShapeDtypeStruct((bsz, seq_len, RET_QK_W), BF16),
                 jax.ShapeDtypeStruct((bsz, seq_len // chunk, RET_QK_W, chunk), BF16),
                 jax.ShapeDtypeStruct((bsz, seq_len, RET_V_W), BF16),
                 jax.ShapeDtypeStruct((bsz, seq_len, RET_V_W), BF16)]
    out_specs = [pl.BlockSpec((1, tm, RET_QK_W), lambda i, b: (b, i, 0)),
                 pl.BlockSpec((1, tm // chunk, RET_QK_W, chunk), lambda i, b: (b, i, 0, 0)),
                 pl.BlockSpec((1, tm, RET_V_W), lambda i, b: (b, i, 0)),
                 pl.BlockSpec((1, tm, RET_V_W), lambda i, b: (b, i, 0))]
    return pl.pallas_call(
        functools.partial(_pre1_body, rope=rope, chunk=chunk),
        grid=(seq_len // tm, bsz),
        in_specs=in_specs, out_specs=out_specs, out_shape=out_shape,
        compiler_params=_params(2),
        name="l1_in_proj",
    )(*inputs)


def _retention_body(*refs, n_chunks, chunk, has_init, emit_state):
    it = iter(refs)
    lgf_ref, lgb_ref, q_ref, kt_ref, v_ref = (next(it) for _ in range(5))
    s0f_ref = next(it) if has_init else None
    s0b_ref = next(it) if has_init else None
    o_ref = next(it)
    sf_out = next(it) if emit_state else None
    sb_out = next(it) if emit_state else None
    sf_ref, sb_ref = next(it), next(it)

    head = pl.program_id(1)
    lg_f = lgf_ref[head]
    lg_b = lgb_ref[head]
    c_f32 = float(chunk)

    i_idx = lax.broadcasted_iota(jnp.int32, (chunk, chunk), 0).astype(F32)
    j_idx = lax.broadcasted_iota(jnp.int32, (chunk, chunk), 1).astype(F32)
    diff = i_idx - j_idx
    decay = (jnp.exp(jnp.where(diff >= 0, diff * lg_f, NEG_INF))
             + jnp.exp(jnp.where(diff <= 0, -diff * lg_b, NEG_INF)))
    row = lax.broadcasted_iota(jnp.int32, (chunk, 1), 0).astype(F32)
    lanes = lax.broadcasted_iota(jnp.int32, (1, chunk), 1).astype(F32)
    q_decay_f = jnp.exp((row + 1.0) * lg_f)
    q_decay_b = jnp.exp((c_f32 - row) * lg_b)
    k_decay_f = jnp.exp((c_f32 - 1.0 - lanes) * lg_f)
    k_decay_b = jnp.exp(lanes * lg_b)
    one = jnp.ones((1, 1), F32)
    chunk_decay_f = jnp.exp(one * (c_f32 * lg_f))
    chunk_decay_b = jnp.exp(one * (c_f32 * lg_b))

    if has_init:
        sf_ref[...] = s0f_ref[0, 0]
        sb_ref[...] = s0b_ref[0, 0]
    else:
        sf_ref[...] = jnp.zeros_like(sf_ref)
        sb_ref[...] = jnp.zeros_like(sb_ref)

    def rows_of(c):
        return pl.ds(pl.multiple_of(c * chunk, chunk), chunk)

    def visit(c, s_ref, q_decay, k_decay, chunk_decay, first):
        qc = q_ref[0, rows_of(c), :]
        ktc = kt_ref[0, c]
        vc = v_ref[0, rows_of(c), :]
        inter = jnp.dot(qc, s_ref[...].astype(BF16), preferred_element_type=F32) * q_decay
        if first:
            att = jnp.dot(qc, ktc, preferred_element_type=F32) * decay
            o_ref[0, rows_of(c), :] = jnp.dot(att.astype(BF16), vc, preferred_element_type=F32) + inter
        else:
            o_ref[0, rows_of(c), :] = o_ref[0, rows_of(c), :] + inter
        s_ref[...] = chunk_decay * s_ref[...] + jnp.dot(
            (ktc.astype(F32) * k_decay).astype(BF16), vc, preferred_element_type=F32)

    def visit_pair(c_up, c_down, first):
        visit(c_up, sf_ref, q_decay_f, k_decay_f, chunk_decay_f, first)
        visit(c_down, sb_ref, q_decay_b, k_decay_b, chunk_decay_b, first)

    half = n_chunks // 2
    if n_chunks % 2:
        raise NotImplementedError("retention kernel needs an even number of chunks")

    def first_half(t, _):
        visit_pair(t, n_chunks - 1 - t, True)
        return 0

    def second_half(t, _):
        visit_pair(half + t, half - 1 - t, False)
        return 0

    lax.fori_loop(0, half, first_half, 0)
    lax.fori_loop(0, half, second_half, 0)
    if emit_state:
        sf_out[0, 0] = sf_ref[...]
        sb_out[0, 0] = sb_ref[...]


def _retention(q, kt, v, lg_f, lg_b, init, *, chunk, emit_state):
    bsz, seq_len, _ = q.shape
    n_chunks = seq_len // chunk
    smem = pl.BlockSpec(memory_space=pltpu.SMEM)
    state_spec = pl.BlockSpec((1, 1, RET_DK, RET_DV), lambda b, h: (b, h, 0, 0))
    inputs = [lg_f, lg_b, q, kt, v]
    in_specs = [smem, smem,
                pl.BlockSpec((1, seq_len, RET_DK), lambda b, h: (b, 0, h)),
                pl.BlockSpec((1, n_chunks, RET_DK, chunk), lambda b, h: (b, 0, h, 0)),
                pl.BlockSpec((1, seq_len, RET_DV), lambda b, h: (b, 0, h))]
    if init is not None:
        inputs += list(init)
        in_specs += [state_spec, state_spec]
    out_shape = [jax.ShapeDtypeStruct((bsz, seq_len, RET_V_W), F32)]
    out_specs = [pl.BlockSpec((1, seq_len, RET_DV), lambda b, h: (b, 0, h))]
    if emit_state:
        out_shape += [jax.ShapeDtypeStruct((bsz, RET_HEADS, RET_DK, RET_DV), F32)] * 2
        out_specs += [state_spec, state_spec]
    return pl.pallas_call(
        functools.partial(_retention_body, n_chunks=n_chunks, chunk=chunk,
                          has_init=init is not None, emit_state=emit_state),
        grid=(bsz, RET_HEADS),
        in_specs=in_specs, out_specs=out_specs, out_shape=out_shape,
        scratch_shapes=[pltpu.VMEM((RET_DK, RET_DV), F32), pltpu.VMEM((RET_DK, RET_DV), F32)],
        compiler_params=_params(2),
        name="l1_retention",
    )(*inputs)


def _layer0(x, mod, norm_mix, norm_mlp, w_in, q_gain, k_gain, sink, w_out, w1, w2, ctx):
    sample = ctx is not None
    outs = _pre0(x, mod, norm_mix, w_in, q_gain, k_gain, rope=sample, emit_state=not sample)
    qk, vt = outs[0], outs[1]
    ctx_a = ctx_s = None
    if sample:
        bsz, past = ctx[0].shape[:2]

        def keys(t):
            return t.reshape(bsz, past, ATTN_KV_W).astype(BF16)

        def values_t(t):
            return jnp.transpose(t, (0, 2, 3, 1)).astype(BF16)

        ctx_a = (keys(ctx[0]), values_t(ctx[1]))
        ctx_s = (keys(ctx[2]), values_t(ctx[3]))
    out_a = _attention(qk, vt, 0, ctx=ctx_a)
    out_s = _attention(qk, vt, 1, ctx=ctx_s, sink=sink, band=sample)
    x = _post0(x, out_a, out_s, mod, w_out, norm_mlp, w1, w2)
    state = ()
    if not sample:
        bsz, seq_len, _ = x.shape
        st = outs[2].reshape(bsz, seq_len, 4, N_KV_HEADS, HEAD_DIM)
        state = tuple(st[:, :, j] for j in range(4))
    return x, state


def _layer1(x, mod, norm_mix, norm_mlp, w_in, lg_f, lg_b, gn_gain, w_out, w1, w2, final_gain, ctx):
    sample = ctx is not None
    seq_len = x.shape[1]
    chunk = RET_CHUNK if seq_len % (2 * RET_CHUNK) == 0 else RET_CHUNK // 2
    q, kt, v, g = _pre1(x, mod, norm_mix, w_in, rope=sample, chunk=chunk)
    outs = _retention(q, kt, v, lg_f, lg_b, ctx, chunk=chunk, emit_state=not sample)
    x = _post1(x, outs[0], g, gn_gain, mod, w_out, norm_mlp, w1, w2, final_gain)
    return x, tuple(outs[1:])


def kernel(x_prompt, x_sample, c, cache_l0_attn_k, cache_l0_attn_v, cache_l0_swa_k, cache_l0_swa_v, state_l1_ret_fwd, state_l1_ret_bwd, c_ctx, l0_ada_w, l0_ada_b, l0_norm_mix, l0_norm_mlp, l0_w_in, l0_q_norm, l0_k_norm, l0_sink, l0_w_out, l0_mlp_w1, l0_mlp_w2, l1_ada_w, l1_ada_b, l1_norm_mix, l1_norm_mlp, l1_w_in, l1_ret_decay_fwd, l1_ret_decay_bwd, l1_ret_gn, l1_w_out, l1_mlp_w1, l1_mlp_w2, final_norm):
    n_dec = c.shape[0]
    pad = (-(n_dec + 1)) % 8
    cond = jnp.concatenate([c, c_ctx[None, :], jnp.zeros((pad, D_MODEL), F32)], axis=0)

    def modulation(w, b):
        mod = _ada(cond, w, b).reshape(cond.shape[0], ADA_CHUNKS, D_MODEL)
        return mod[n_dec:n_dec + 1], mod[:n_dec]

    mod0_p, mod0_s = modulation(l0_ada_w, l0_ada_b)
    mod1_p, mod1_s = modulation(l1_ada_w, l1_ada_b)

    l0 = (l0_norm_mix, l0_norm_mlp, l0_w_in.astype(BF16), l0_q_norm, l0_k_norm, l0_sink,
          l0_w_out.astype(BF16), l0_mlp_w1.astype(BF16), l0_mlp_w2.astype(BF16))
    x_p, st0 = _layer0(x_prompt, mod0_p, *l0, ctx=None)
    x_s, _ = _layer0(x_sample, mod0_s, *l0,
                     ctx=(cache_l0_attn_k, cache_l0_attn_v, cache_l0_swa_k, cache_l0_swa_v))

    l1 = (l1_norm_mix, l1_norm_mlp, l1_w_in.astype(BF16),
          jax.nn.log_sigmoid(l1_ret_decay_fwd.astype(F32)), jax.nn.log_sigmoid(l1_ret_decay_bwd.astype(F32)),
          l1_ret_gn, l1_w_out.astype(BF16), l1_mlp_w1.astype(BF16), l1_mlp_w2.astype(BF16), final_norm)
    y_p, st1 = _layer1(x_p, mod1_p, *l1, ctx=None)
    y_s, _ = _layer1(x_s, mod1_s, *l1, ctx=(state_l1_ret_fwd, state_l1_ret_bwd))
    return (y_p, y_s) + st0 + st1
```

```python
import functools

import jax
import jax.numpy as jnp
from jax import lax
from jax.experimental import pallas as pl
from jax.experimental.pallas import tpu as pltpu

F32 = jnp.float32
BF16 = jnp.bfloat16

LANES = 128
D_MODEL = 1024
GRID_W = 64
HEAD_DIM = 64
N_Q_HEADS = 8
N_KV_HEADS = 2
GQA = N_Q_HEADS // N_KV_HEADS
WINDOW = 128
Q_BLOCK = 128
ROPE_THETA = 10000.0
RET_DK = 256
RET_DV = 512
RET_HEADS = 4
D_FF = 4 * D_MODEL
EPS = 1e-6
NEG_INF = -1e30
ADA_CHUNKS = 6

ATTN_Q_W = N_Q_HEADS * HEAD_DIM
ATTN_KV_W = N_KV_HEADS * HEAD_DIM
ATTN_IN_W = 2 * ATTN_Q_W + 4 * ATTN_KV_W
QK_W = 2 * ATTN_Q_W + 2 * ATTN_KV_W
LOG2_E = 1.4426950408889634
Q_SCALE = HEAD_DIM ** -0.5 * LOG2_E
DENOM_ROWS = 16
RET_QK_W = RET_HEADS * RET_DK
RET_V_W = RET_HEADS * RET_DV

VMEM_LIMIT_BYTES = 56 * 1024 * 1024
ROW_TILE = 512
ATTN_KEY_CHUNK = 256
KEY_SUB = 256
FULL_Q_BLOCKS = 1
BAND_Q_BLOCKS = 4
FF_CHUNK = 1024
RET_CHUNK = 256


def _resident(shape):
    n = len(shape)
    return pl.BlockSpec(shape, lambda *_: (0,) * n, pipeline_mode=pl.Buffered(1))


def _params(n_axes):
    return pltpu.CompilerParams(
        dimension_semantics=("arbitrary",) * n_axes,
        vmem_limit_bytes=VMEM_LIMIT_BYTES)


def _silu(x):
    return x / (1.0 + jnp.exp(-x))


def _rms(x, gain):
    return x * lax.rsqrt(jnp.mean(x * x, axis=-1, keepdims=True) + EPS) * gain


def _modulated_norm(x, gain, scale, shift):
    return _rms(x, gain) * (1.0 + scale) + shift


def _swap_halves(x, s):
    w = x.shape[-1]
    lane = lax.broadcasted_iota(jnp.int32, x.shape, x.ndim - 1)
    fwd = pltpu.roll(x, w - s, x.ndim - 1)
    bwd = pltpu.roll(x, s, x.ndim - 1)
    return jnp.where((lane & s) == 0, fwd, bwd)


def _rope(x, cos, sin_signed, s):
    reps = x.shape[-1] // cos.shape[-1]
    if reps > 1:
        cos = jnp.concatenate([cos] * reps, axis=-1)
        sin_signed = jnp.concatenate([sin_signed] * reps, axis=-1)
    return x * cos + _swap_halves(x, s) * sin_signed


def _rope_tables(seq_len, head_dim):
    n_heads = max(1, LANES // head_dim)
    rows = seq_len // GRID_W
    row = jnp.repeat(jnp.arange(rows, dtype=F32), GRID_W)
    col = jnp.tile(jnp.arange(GRID_W, dtype=F32), rows)
    n_freq = head_dim // 4
    inv = ROPE_THETA ** (-jnp.arange(n_freq, dtype=F32) / n_freq)
    ang_r = row[:, None] * inv[None, :]
    ang_c = col[:, None] * inv[None, :]
    cos = jnp.concatenate([jnp.cos(ang_r)] * 2 + [jnp.cos(ang_c)] * 2, axis=-1)
    sin = jnp.concatenate([-jnp.sin(ang_r), jnp.sin(ang_r), -jnp.sin(ang_c), jnp.sin(ang_c)], axis=-1)
    return jnp.tile(cos, (1, n_heads)), jnp.tile(sin, (1, n_heads))


def _ada_body(cond_ref, w_ref, b_ref, o_ref):
    a = _silu(cond_ref[...]).astype(BF16)
    o_ref[...] = jnp.dot(a, w_ref[...].astype(BF16), preferred_element_type=F32) + b_ref[...]


def _ada(cond, w, b):
    rows = cond.shape[0]
    n = w.shape[1]
    return pl.pallas_call(
        _ada_body,
        grid=(n // D_MODEL,),
        in_specs=[pl.BlockSpec((rows, D_MODEL), lambda j: (0, 0)),
                  pl.BlockSpec((D_MODEL, D_MODEL), lambda j: (0, j)),
                  pl.BlockSpec((1, D_MODEL), lambda j: (0, j))],
        out_specs=pl.BlockSpec((rows, D_MODEL), lambda j: (0, j)),
        out_shape=jax.ShapeDtypeStruct((rows, n), F32),
        compiler_params=_params(1),
        name="ada_modulation",
    )(cond, w, b.reshape(1, n))


def _head_rms(x, avg, gain):
    sq = x * x
    hi = sq.astype(BF16)
    lo = (sq - hi.astype(F32)).astype(BF16)
    ms = (jnp.dot(hi, avg, preferred_element_type=F32)
          + jnp.dot(lo, avg, preferred_element_type=F32))
    return x * lax.rsqrt(ms + EPS) * gain


def _pre0_body(*refs, rope, emit_state):
    it = iter(refs)
    x_ref, mod_ref, gain_ref, w_ref, avg_ref, qg_ref, kg_ref = (next(it) for _ in range(7))
    cos_ref = next(it) if rope else None
    sin_ref = next(it) if rope else None
    qk_ref, vt_ref = next(it), next(it)
    state_ref = next(it) if emit_state else None

    h = _modulated_norm(x_ref[0], gain_ref[...], mod_ref[0, 1:2, :], mod_ref[0, 0:1, :]).astype(BF16)

    def proj(lo, width):
        return jnp.dot(h, w_ref[:, lo:lo + width], preferred_element_type=F32)

    qa = _head_rms(proj(0, ATTN_Q_W), avg_ref[...], qg_ref[...])
    ka = _head_rms(proj(ATTN_Q_W, ATTN_KV_W), avg_ref[:ATTN_KV_W, :ATTN_KV_W], kg_ref[...])
    va = proj(ATTN_Q_W + ATTN_KV_W, ATTN_KV_W)
    base = ATTN_Q_W + 2 * ATTN_KV_W
    qs = proj(base, ATTN_Q_W)
    ks = proj(base + ATTN_Q_W, ATTN_KV_W)
    vs = proj(base + ATTN_Q_W + ATTN_KV_W, ATTN_KV_W)
    if emit_state:
        state_ref[0] = jnp.concatenate([ka, va, ks, vs], axis=-1)
    if rope:
        cos, sin = cos_ref[...], sin_ref[...]
        s = HEAD_DIM // 4
        qa = _rope(qa, cos, sin, s)
        qs = _rope(qs, cos, sin, s)
        ka = _rope(ka, cos, sin, s)
        ks = _rope(ks, cos, sin, s)
    qk_ref[0] = jnp.concatenate(
        [qa * Q_SCALE, qs * Q_SCALE, ka, ks], axis=-1).astype(BF16)
    vat = va.T.astype(BF16)
    vst = vs.T.astype(BF16)
    for j in range(N_KV_HEADS):
        vt_ref[0, j] = vat[j * HEAD_DIM:(j + 1) * HEAD_DIM, :]
        vt_ref[0, N_KV_HEADS + j] = vst[j * HEAD_DIM:(j + 1) * HEAD_DIM, :]


def _pre0(x, mod, gain, w_in, q_gain, k_gain, *, rope, emit_state):
    bsz, seq_len, _ = x.shape
    tm = min(ROW_TILE, seq_len)
    per_batch_mod = mod.shape[0] > 1
    head = lax.broadcasted_iota(jnp.int32, (ATTN_Q_W, ATTN_Q_W), 0) // HEAD_DIM
    avg = jnp.where(head == head.T, 1.0 / HEAD_DIM, 0.0).astype(BF16)
    inputs = [x, mod, gain.reshape(1, D_MODEL), w_in, avg,
              jnp.tile(q_gain, N_Q_HEADS).reshape(1, ATTN_Q_W),
              jnp.tile(k_gain, N_KV_HEADS).reshape(1, ATTN_KV_W)]
    in_specs = [pl.BlockSpec((1, tm, D_MODEL), lambda i, b: (b, i, 0)),
                pl.BlockSpec((1, ADA_CHUNKS, D_MODEL),
                             (lambda i, b: (b, 0, 0)) if per_batch_mod else (lambda i, b: (0, 0, 0))),
                _resident((1, D_MODEL)), _resident((D_MODEL, ATTN_IN_W)),
                _resident((ATTN_Q_W, ATTN_Q_W)), _resident((1, ATTN_Q_W)), _resident((1, ATTN_KV_W))]
    if rope:
        cos, sin = _rope_tables(seq_len, HEAD_DIM)
        inputs += [cos, sin]
        in_specs += [pl.BlockSpec((tm, cos.shape[1]), lambda i, b: (i, 0))] * 2
    out_shape = [jax.ShapeDtypeStruct((bsz, seq_len, QK_W), BF16),
                 jax.ShapeDtypeStruct((bsz, 2 * N_KV_HEADS, HEAD_DIM, seq_len), BF16)]
    out_specs = [pl.BlockSpec((1, tm, QK_W), lambda i, b: (b, i, 0)),
                 pl.BlockSpec((1, 2 * N_KV_HEADS, HEAD_DIM, tm), lambda i, b: (b, 0, 0, i))]
    if emit_state:
        out_shape.append(jax.ShapeDtypeStruct((bsz, seq_len, 4 * ATTN_KV_W), F32))
        out_specs.append(pl.BlockSpec((1, tm, 4 * ATTN_KV_W), lambda i, b: (b, i, 0)))
    return pl.pallas_call(
        functools.partial(_pre0_body, rope=rope, emit_state=emit_state),
        grid=(seq_len // tm, bsz),
        in_specs=in_specs, out_specs=out_specs, out_shape=out_shape,
        compiler_params=_params(2),
        name="l0_in_proj",
    )(*inputs)


def _attn_body(*refs, seq_len, q_blocks, has_ctx, band, has_sink):
    it = iter(refs)
    sink_ref = next(it) if has_sink else None
    q_ref, k_ref, vt_ref = next(it), next(it), next(it)
    ck_ref = next(it) if has_ctx else None
    cvt_ref = next(it) if has_ctx else None
    bias_ref = next(it) if band else None
    o_ref = next(it)

    tq = Q_BLOCK
    cols = GQA * tq
    streams = [(j, h) for j in range(q_blocks) for h in range(N_KV_HEADS)]

    qt = q_ref[0].astype(F32).T
    qpt = {}
    for j, h in streams:
        rows = [qt[(h * GQA + g) * HEAD_DIM:(h * GQA + g + 1) * HEAD_DIM, j * tq:(j + 1) * tq]
                for g in range(GQA)]
        qh = jnp.concatenate(rows, axis=1)
        halves = [jnp.zeros_like(qh)] * N_KV_HEADS
        halves[h] = qh
        qpt[j, h] = jnp.concatenate(halves, axis=0).astype(BF16)

    m, acc = {}, {}
    for j, h in streams:
        if has_sink:
            group = lax.broadcasted_iota(jnp.int32, (1, cols), 1) // tq
            m_h = jnp.zeros((1, cols), F32)
            for g in range(GQA):
                m_h = jnp.where(group == g, sink_ref[h * GQA + g] * LOG2_E, m_h)
            denom0 = 1.0
        else:
            m_h = jnp.full((1, cols), NEG_INF, F32)
            denom0 = 0.0
        m[j, h] = m_h
        acc[j, h] = jnp.concatenate([jnp.zeros((HEAD_DIM, cols), F32),
                                     jnp.full((DENOM_ROWS, cols), denom0, F32)], axis=0)

    def static_subs(k_at, vt_at, lo, n_keys):
        return [(KEY_SUB, functools.partial(k_at, lo + o), functools.partial(vt_at, lo + o), None)
                for o in range(0, n_keys, KEY_SUB)]

    shared = []
    if has_ctx:
        past = ck_ref.shape[1]
        tk = min(ATTN_KEY_CHUNK, past)
        for c in range(past // tk):
            shared.append(static_subs(lambda o: ck_ref[0, o:o + KEY_SUB, :],
                                      lambda o, h: cvt_ref[0, h, :, o:o + KEY_SUB], c * tk, tk))
    if not band:
        tk = min(ATTN_KEY_CHUNK, seq_len)
        for c in range(seq_len // tk):
            shared.append(static_subs(lambda o: k_ref[0, o:o + KEY_SUB, :],
                                      lambda o, h: vt_ref[0, h, :, o:o + KEY_SUB], c * tk, tk))
    chunks = {j: list(shared) for j in range(q_blocks)}
    if band:
        width = Q_BLOCK + 2 * WINDOW
        for j in range(q_blocks):
            qi = pl.program_id(1) * q_blocks + j
            start = jnp.clip(qi * tq - WINDOW, 0, seq_len - width)
            variant = jnp.where(qi == 0, 0, jnp.where(qi == seq_len // tq - 1, 2, 1))

            def window_rows(o, n, start=start):
                return pl.ds(pl.multiple_of(start + o, Q_BLOCK), n)

            for o in range(0, width, KEY_SUB):
                n = min(KEY_SUB, width - o)
                chunks[j].append([(
                    n,
                    functools.partial(lambda o, n, rows: k_ref[0, rows(o, n), :], o, n, window_rows),
                    functools.partial(lambda o, n, rows, h: vt_ref[0, h, :, rows(o, n)], o, n, window_rows),
                    functools.partial(lambda o, n, v: bias_ref[v, o:o + n, :], o, n, variant))])
    n_chunks = len(chunks[0])

    def scores(sub, key):
        _, keys, _, bias = sub
        st = jnp.dot(keys(), qpt[key], preferred_element_type=F32)
        return st if bias is None else st + bias()

    def col_max(a, b):
        b = jnp.max(b, axis=0, keepdims=True)
        return b if a is None else jnp.maximum(a, b)

    def rescale(key, chunk_max):
        m_new = jnp.maximum(m[key], chunk_max)
        acc[key] = acc[key] * jnp.exp2(m[key] - m_new)
        m[key] = m_new

    cur = {key: [] for key in streams}
    chunk_max = {key: None for key in streams}
    for s in range(len(chunks[0][0])):
        for key in streams:
            cur[key].append(scores(chunks[key[0]][0][s], key))
            chunk_max[key] = col_max(chunk_max[key], cur[key][-1])
    for key in streams:
        rescale(key, chunk_max[key])
    for c in range(n_chunks):
        n_cur = len(chunks[0][c])
        n_next = len(chunks[0][c + 1]) if c + 1 < n_chunks else 0
        nxt = {key: [] for key in streams}
        chunk_max = {key: None for key in streams}
        for s in range(max(n_cur, n_next)):
            for key in streams:
                j, h = key
                if s < n_next:
                    nxt[key].append(scores(chunks[j][c + 1][s], key))
                    chunk_max[key] = col_max(chunk_max[key], nxt[key][-1])
                if s < n_cur:
                    n_keys, _, values_t, _ = chunks[j][c][s]
                    pt = jnp.exp2(cur[key][s] - m[key]).astype(BF16)
                    lhs = jnp.concatenate([values_t(h), jnp.ones((DENOM_ROWS, n_keys), BF16)], axis=0)
                    acc[key] = acc[key] + jnp.dot(lhs, pt, preferred_element_type=F32)
        if n_next:
            for key in streams:
                rescale(key, chunk_max[key])
            cur = nxt

    for j in range(q_blocks):
        pieces = []
        for h in range(N_KV_HEADS):
            out_t = acc[j, h][:HEAD_DIM, :] / acc[j, h][HEAD_DIM:HEAD_DIM + 1, :]
            pieces += [out_t[:, g * tq:(g + 1) * tq] for g in range(GQA)]
        o_ref[0, j * tq:(j + 1) * tq, :] = jnp.concatenate(pieces, axis=0).T.astype(o_ref.dtype)


def _band_bias():
    width = Q_BLOCK + 2 * WINDOW
    r = lax.broadcasted_iota(jnp.int32, (width, GQA * Q_BLOCK), 0)
    t = lax.broadcasted_iota(jnp.int32, (width, GQA * Q_BLOCK), 1) % Q_BLOCK
    rel = jnp.stack([r - t, r - t - WINDOW, r - t - 2 * WINDOW])
    return jnp.where(jnp.abs(rel) <= WINDOW, 0.0, NEG_INF).astype(F32)


def _attention(qk, vt, group, *, q_blocks, ctx=None, sink=None, band=False):
    bsz, seq_len, _ = qk.shape
    k_col = 2 * ATTN_Q_W // ATTN_KV_W + group
    inputs, in_specs = [], []
    if sink is not None:
        inputs.append(sink)
        in_specs.append(pl.BlockSpec(memory_space=pltpu.SMEM))
    inputs += [qk, qk, vt]
    tq = q_blocks * Q_BLOCK
    in_specs += [pl.BlockSpec((1, tq, ATTN_Q_W), lambda b, i: (b, i, group)),
                 pl.BlockSpec((1, seq_len, ATTN_KV_W), lambda b, i: (b, 0, k_col)),
                 pl.BlockSpec((1, N_KV_HEADS, HEAD_DIM, seq_len), lambda b, i: (b, group, 0, 0))]
    if ctx is not None:
        past = ctx[0].shape[1]
        inputs += list(ctx)
        in_specs += [pl.BlockSpec((1, past, ATTN_KV_W), lambda b, i: (b, 0, 0)),
                     pl.BlockSpec((1, N_KV_HEADS, HEAD_DIM, past), lambda b, i: (b, 0, 0, 0))]
    if band:
        assert seq_len >= Q_BLOCK + 2 * WINDOW
        bias = _band_bias()
        inputs.append(bias)
        in_specs.append(_resident(bias.shape))
    return pl.pallas_call(
        functools.partial(_attn_body, seq_len=seq_len, q_blocks=q_blocks, has_ctx=ctx is not None,
                          band=band, has_sink=sink is not None),
        grid=(bsz, seq_len // tq),
        in_specs=in_specs,
        out_specs=pl.BlockSpec((1, tq, ATTN_Q_W), lambda b, i: (b, i, 0)),
        out_shape=jax.ShapeDtypeStruct((bsz, seq_len, ATTN_Q_W), BF16),
        compiler_params=_params(2),
        name="l0_attention_band" if band else "l0_attention_full",
    )(*inputs)


def _mlp_tail(x, y, mod_ref, nmlp_ref, w1_ref, w2_ref, fin_ref):
    x = x + mod_ref[0, 2:3, :] * y
    h = _modulated_norm(x, nmlp_ref[...], mod_ref[0, 4:5, :], mod_ref[0, 3:4, :]).astype(BF16)
    ff = jnp.zeros_like(x)
    for j in range(D_FF // FF_CHUNK):
        cols = slice(j * FF_CHUNK, (j + 1) * FF_CHUNK)
        hid = jnp.maximum(jnp.dot(h, w1_ref[:, cols], preferred_element_type=F32), 0.0)
        ff = ff + jnp.dot((hid * hid).astype(BF16), w2_ref[cols, :], preferred_element_type=F32)
    x = x + mod_ref[0, 5:6, :] * ff
    if fin_ref is not None:
        x = _rms(x, fin_ref[...])
    return x


def _post0_body(*refs, final):
    it = iter(refs)
    x_ref, oa_ref, os_ref, mod_ref, wo_ref, nmlp_ref, w1_ref, w2_ref = (next(it) for _ in range(8))
    fin_ref = next(it) if final else None
    o_ref = next(it)
    y = (jnp.dot(oa_ref[0], wo_ref[:ATTN_Q_W, :], preferred_element_type=F32)
         + jnp.dot(os_ref[0], wo_ref[ATTN_Q_W:, :], preferred_element_type=F32))
    o_ref[0] = _mlp_tail(x_ref[0], y, mod_ref, nmlp_ref, w1_ref, w2_ref, fin_ref)


def _post1_body(*refs, final):
    it = iter(refs)
    x_ref, o_in_ref, g_ref, gn_ref, mod_ref, wo_ref, nmlp_ref, w1_ref, w2_ref = (next(it) for _ in range(9))
    fin_ref = next(it) if final else None
    o_ref = next(it)
    y = jnp.zeros(x_ref.shape[1:], F32)
    for h in range(RET_HEADS):
        cols = slice(h * RET_DV, (h + 1) * RET_DV)
        normed = _rms(o_in_ref[0, :, cols], gn_ref[:, cols])
        gated = (_silu(g_ref[0, :, cols].astype(F32)) * normed).astype(BF16)
        y = y + jnp.dot(gated, wo_ref[cols, :], preferred_element_type=F32)
    o_ref[0] = _mlp_tail(x_ref[0], y, mod_ref, nmlp_ref, w1_ref, w2_ref, fin_ref)


def _post(body, x, mixer_inputs, mixer_specs, mod, w_out, norm_mlp, w1, w2, final_gain, name):
    bsz, seq_len, _ = x.shape
    tm = min(ROW_TILE, seq_len)
    per_batch_mod = mod.shape[0] > 1
    row_spec = pl.BlockSpec((1, tm, D_MODEL), lambda i, b: (b, i, 0))
    inputs = [x] + list(mixer_inputs) + [mod, w_out, norm_mlp.reshape(1, D_MODEL), w1, w2]
    in_specs = [row_spec] + list(mixer_specs(tm)) + [
        pl.BlockSpec((1, ADA_CHUNKS, D_MODEL),
                     (lambda i, b: (b, 0, 0)) if per_batch_mod else (lambda i, b: (0, 0, 0))),
        _resident(w_out.shape), _resident((1, D_MODEL)), _resident(w1.shape), _resident(w2.shape)]
    if final_gain is not None:
        inputs.append(final_gain.reshape(1, D_MODEL))
        in_specs.append(_resident((1, D_MODEL)))
    return pl.pallas_call(
        functools.partial(body, final=final_gain is not None),
        grid=(seq_len // tm, bsz),
        in_specs=in_specs, out_specs=row_spec,
        out_shape=jax.ShapeDtypeStruct(x.shape, F32),
        compiler_params=_params(2),
        name=name,
    )(*inputs)


def _post0(x, oa, os_, mod, w_out, norm_mlp, w1, w2):
    def specs(tm):
        return [pl.BlockSpec((1, tm, ATTN_Q_W), lambda i, b: (b, i, 0))] * 2
    return _post(_post0_body, x, [oa, os_], specs, mod, w_out, norm_mlp, w1, w2, None, "l0_out_proj_mlp")


def _post1(x, o_ret, g, gn_gain, mod, w_out, norm_mlp, w1, w2, final_gain):
    def specs(tm):
        return [pl.BlockSpec((1, tm, RET_V_W), lambda i, b: (b, i, 0)),
                pl.BlockSpec((1, tm, RET_V_W), lambda i, b: (b, i, 0)),
                _resident((1, RET_V_W))]
    return _post(_post1_body, x, [o_ret, g, gn_gain.reshape(1, RET_V_W)], specs, mod, w_out, norm_mlp,
                 w1, w2, final_gain, "l1_out_proj_mlp")


def _pre1_body(*refs, rope, chunk):
    it = iter(refs)
    x_ref, mod_ref, gain_ref, w_ref = (next(it) for _ in range(4))
    cos_ref = next(it) if rope else None
    sin_ref = next(it) if rope else None
    q_ref, kt_ref, v_ref, g_ref = (next(it) for _ in range(4))

    h = _modulated_norm(x_ref[0], gain_ref[...], mod_ref[0, 1:2, :], mod_ref[0, 0:1, :]).astype(BF16)

    def proj(lo, width):
        return jnp.dot(h, w_ref[:, lo:lo + width], preferred_element_type=F32)

    q = proj(0, RET_QK_W)
    k = proj(RET_QK_W, RET_QK_W)
    if rope:
        cos, sin = cos_ref[...], sin_ref[...]
        q = _rope(q, cos, sin, RET_DK // 4)
        k = _rope(k, cos, sin, RET_DK // 4)
    q_ref[0] = q.astype(BF16)
    k = k * (RET_DK ** -0.5)
    for j in range(k.shape[0] // chunk):
        kt_ref[0, j] = k[j * chunk:(j + 1) * chunk, :].T.astype(BF16)
    for j in range(RET_V_W // RET_QK_W):
        cols = slice(j * RET_QK_W, (j + 1) * RET_QK_W)
        v_ref[0, :, cols] = proj(2 * RET_QK_W + j * RET_QK_W, RET_QK_W).astype(BF16)
        g_ref[0, :, cols] = proj(2 * RET_QK_W + RET_V_W + j * RET_QK_W, RET_QK_W).astype(BF16)


def _pre1(x, mod, gain, w_in, *, rope, chunk):
    bsz, seq_len, _ = x.shape
    tm = min(ROW_TILE, seq_len)
    per_batch_mod = mod.shape[0] > 1
    inputs = [x, mod, gain.reshape(1, D_MODEL), w_in]
    in_specs = [pl.BlockSpec((1, tm, D_MODEL), lambda i, b: (b, i, 0)),
                pl.BlockSpec((1, ADA_CHUNKS, D_MODEL),
                             (lambda i, b: (b, 0, 0)) if per_batch_mod else (lambda i, b: (0, 0, 0))),
                _resident((1, D_MODEL)), _resident(w_in.shape)]
    if rope:
        cos, sin = _rope_tables(seq_len, RET_DK)
        inputs += [cos, sin]
        in_specs += [pl.BlockSpec((tm, cos.shape[1]), lambda i, b: (i, 0))] * 2
    out_shape = [jax.ShapeDtypeStruct((bsz, seq_len, RET_QK_W), BF16),
                 jax.ShapeDtypeStruct((bsz, seq_len // chunk, RET_QK_W, chunk), BF16),
                 jax.ShapeDtypeStruct((bsz, seq_len, RET_V_W), BF16),
                 jax.ShapeDtypeStruct((bsz, seq_len, RET_V_W), BF16)]
    out_specs = [pl.BlockSpec((1, tm, RET_QK_W), lambda i, b: (b, i, 0)),
                 pl.BlockSpec((1, tm // chunk, RET_QK_W, chunk), lambda i, b: (b, i, 0, 0)),
                 pl.BlockSpec((1, tm, RET_V_W), lambda i, b: (b, i, 0)),
                 pl.BlockSpec((1, tm, RET_V_W), lambda i, b: (b, i, 0))]
    return pl.pallas_call(
        functools.partial(_pre1_body, rope=rope, chunk=chunk),
        grid=(seq_len // tm, bsz),
        in_specs=in_specs, out_specs=out_specs, out_shape=out_shape,
        compiler_params=_params(2),
        name="l1_in_proj",
    )(*inputs)


def _retention_body(*refs, n_chunks, chunk, has_init, emit_state):
    it = iter(refs)
    lgf_ref, lgb_ref, q_ref, kt_ref, v_ref = (next(it) for _ in range(5))
    s0f_ref = next(it) if has_init else None
    s0b_ref = next(it) if has_init else None
    o_ref = next(it)
    sf_out = next(it) if emit_state else None
    sb_out = next(it) if emit_state else None
    sf_ref, sb_ref = next(it), next(it)

    head = pl.program_id(1)
    lg_f = lgf_ref[head]
    lg_b = lgb_ref[head]
    c_f32 = float(chunk)

    i_idx = lax.broadcasted_iota(jnp.int32, (chunk, chunk), 0).astype(F32)
    j_idx = lax.broadcasted_iota(jnp.int32, (chunk, chunk), 1).astype(F32)
    diff = i_idx - j_idx
    decay = (jnp.exp(jnp.where(diff >= 0, diff * lg_f, NEG_INF))
             + jnp.exp(jnp.where(diff <= 0, -diff * lg_b, NEG_INF)))
    row = lax.broadcasted_iota(jnp.int32, (chunk, 1), 0).astype(F32)
    lanes = lax.broadcasted_iota(jnp.int32, (1, chunk), 1).astype(F32)
    q_decay_f = jnp.exp((row + 1.0) * lg_f)
    q_decay_b = jnp.exp((c_f32 - row) * lg_b)
    k_decay_f = jnp.exp((c_f32 - 1.0 - lanes) * lg_f)
    k_decay_b = jnp.exp(lanes * lg_b)
    one = jnp.ones((1, 1), F32)
    chunk_decay_f = jnp.exp(one * (c_f32 * lg_f))
    chunk_decay_b = jnp.exp(one * (c_f32 * lg_b))

    if has_init:
        sf_ref[...] = s0f_ref[0, 0]
        sb_ref[...] = s0b_ref[0, 0]
    else:
        sf_ref[...] = jnp.zeros_like(sf_ref)
        sb_ref[...] = jnp.zeros_like(sb_ref)

    def rows_of(c):
        return pl.ds(pl.multiple_of(c * chunk, chunk), chunk)

    def visit(c, s_ref, q_decay, k_decay, chunk_decay, first):
        qc = q_ref[0, rows_of(c), :]
        ktc = kt_ref[0, c]
        vc = v_ref[0, rows_of(c), :]
        inter = jnp.dot(qc, s_ref[...].astype(BF16), preferred_element_type=F32) * q_decay
        if first:
            att = jnp.dot(qc, ktc, preferred_element_type=F32) * decay
            o_ref[0, rows_of(c), :] = jnp.dot(att.astype(BF16), vc, preferred_element_type=F32) + inter
        else:
            o_ref[0, rows_of(c), :] = o_ref[0, rows_of(c), :] + inter
        s_ref[...] = chunk_decay * s_ref[...] + jnp.dot(
            (ktc.astype(F32) * k_decay).astype(BF16), vc, preferred_element_type=F32)

    def visit_pair(c_up, c_down, first):
        visit(c_up, sf_ref, q_decay_f, k_decay_f, chunk_decay_f, first)
        visit(c_down, sb_ref, q_decay_b, k_decay_b, chunk_decay_b, first)

    half = n_chunks // 2
    if n_chunks % 2:
        raise NotImplementedError("retention kernel needs an even number of chunks")

    def first_half(t, _):
        visit_pair(t, n_chunks - 1 - t, True)
        return 0

    def second_half(t, _):
        visit_pair(half + t, half - 1 - t, False)
        return 0

    lax.fori_loop(0, half, first_half, 0, unroll=True)
    lax.fori_loop(0, half, second_half, 0, unroll=True)
    if emit_state:
        sf_out[0, 0] = sf_ref[...]
        sb_out[0, 0] = sb_ref[...]


def _retention(q, kt, v, lg_f, lg_b, init, *, chunk, emit_state):
    bsz, seq_len, _ = q.shape
    n_chunks = seq_len // chunk
    smem = pl.BlockSpec(memory_space=pltpu.SMEM)
    state_spec = pl.BlockSpec((1, 1, RET_DK, RET_DV), lambda b, h: (b, h, 0, 0))
    inputs = [lg_f, lg_b, q, kt, v]
    in_specs = [smem, smem,
                pl.BlockSpec((1, seq_len, RET_DK), lambda b, h: (b, 0, h)),
                pl.BlockSpec((1, n_chunks, RET_DK, chunk), lambda b, h: (b, 0, h, 0)),
                pl.BlockSpec((1, seq_len, RET_DV), lambda b, h: (b, 0, h))]
    if init is not None:
        inputs += list(init)
        in_specs += [state_spec, state_spec]
    out_shape = [jax.ShapeDtypeStruct((bsz, seq_len, RET_V_W), F32)]
    out_specs = [pl.BlockSpec((1, seq_len, RET_DV), lambda b, h: (b, 0, h))]
    if emit_state:
        out_shape += [jax.ShapeDtypeStruct((bsz, RET_HEADS, RET_DK, RET_DV), F32)] * 2
        out_specs += [state_spec, state_spec]
    return pl.pallas_call(
        functools.partial(_retention_body, n_chunks=n_chunks, chunk=chunk,
                          has_init=init is not None, emit_state=emit_state),
        grid=(bsz, RET_HEADS),
        in_specs=in_specs, out_specs=out_specs, out_shape=out_shape,
        scratch_shapes=[pltpu.VMEM((RET_DK, RET_DV), F32), pltpu.VMEM((RET_DK, RET_DV), F32)],
        compiler_params=_params(2),
        name="l1_retention",
    )(*inputs)


def _layer0(x, mod, norm_mix, norm_mlp, w_in, q_gain, k_gain, sink, w_out, w1, w2, ctx):
    sample = ctx is not None
    outs = _pre0(x, mod, norm_mix, w_in, q_gain, k_gain, rope=sample, emit_state=not sample)
    qk, vt = outs[0], outs[1]
    ctx_a = ctx_s = None
    if sample:
        bsz, past = ctx[0].shape[:2]

        def keys(t):
            return t.reshape(bsz, past, ATTN_KV_W).astype(BF16)

        def values_t(t):
            return jnp.transpose(t, (0, 2, 3, 1)).astype(BF16)

        ctx_a = (keys(ctx[0]), values_t(ctx[1]))
        ctx_s = (keys(ctx[2]), values_t(ctx[3]))
    out_a = _attention(qk, vt, 0, q_blocks=FULL_Q_BLOCKS, ctx=ctx_a)
    out_s = _attention(qk, vt, 1, q_blocks=BAND_Q_BLOCKS if sample else FULL_Q_BLOCKS,
                       ctx=ctx_s, sink=sink, band=sample)
    x = _post0(x, out_a, out_s, mod, w_out, norm_mlp, w1, w2)
    state = ()
    if not sample:
        bsz, seq_len, _ = x.shape
        st = outs[2].reshape(bsz, seq_len, 4, N_KV_HEADS, HEAD_DIM)
        state = tuple(st[:, :, j] for j in range(4))
    return x, state


def _layer1(x, mod, norm_mix, norm_mlp, w_in, lg_f, lg_b, gn_gain, w_out, w1, w2, final_gain, ctx):
    sample = ctx is not None
    seq_len = x.shape[1]
    chunk = RET_CHUNK if seq_len % (2 * RET_CHUNK) == 0 else RET_CHUNK // 2
    q, kt, v, g = _pre1(x, mod, norm_mix, w_in, rope=sample, chunk=chunk)
    outs = _retention(q, kt, v, lg_f, lg_b, ctx, chunk=chunk, emit_state=not sample)
    x = _post1(x, outs[0], g, gn_gain, mod, w_out, norm_mlp, w1, w2, final_gain)
    return x, tuple(outs[1:])


def kernel(x_prompt, x_sample, c, cache_l0_attn_k, cache_l0_attn_v, cache_l0_swa_k, cache_l0_swa_v, state_l1_ret_fwd, state_l1_ret_bwd, c_ctx, l0_ada_w, l0_ada_b, l0_norm_mix, l0_norm_mlp, l0_w_in, l0_q_norm, l0_k_norm, l0_sink, l0_w_out, l0_mlp_w1, l0_mlp_w2, l1_ada_w, l1_ada_b, l1_norm_mix, l1_norm_mlp, l1_w_in, l1_ret_decay_fwd, l1_ret_decay_bwd, l1_ret_gn, l1_w_out, l1_mlp_w1, l1_mlp_w2, final_norm):
    n_dec = c.shape[0]
    pad = (-(n_dec + 1)) % 8
    cond = jnp.concatenate([c, c_ctx[None, :], jnp.zeros((pad, D_MODEL), F32)], axis=0)

    def modulation(w, b):
        mod = _ada(cond, w, b).reshape(cond.shape[0], ADA_CHUNKS, D_MODEL)
        return mod[n_dec:n_dec + 1], mod[:n_dec]

    mod0_p, mod0_s = modulation(l0_ada_w, l0_ada_b)
    mod1_p, mod1_s = modulation(l1_ada_w, l1_ada_b)

    l0 = (l0_norm_mix, l0_norm_mlp, l0_w_in.astype(BF16), l0_q_norm, l0_k_norm, l0_sink,
          l0_w_out.astype(BF16), l0_mlp_w1.astype(BF16), l0_mlp_w2.astype(BF16))
    x_p, st0 = _layer0(x_prompt, mod0_p, *l0, ctx=None)
    x_s, _ = _layer0(x_sample, mod0_s, *l0,
                     ctx=(cache_l0_attn_k, cache_l0_attn_v, cache_l0_swa_k, cache_l0_swa_v))

    l1 = (l1_norm_mix, l1_norm_mlp, l1_w_in.astype(BF16),
          jax.nn.log_sigmoid(l1_ret_decay_fwd.astype(F32)), jax.nn.log_sigmoid(l1_ret_decay_bwd.astype(F32)),
          l1_ret_gn, l1_w_out.astype(BF16), l1_mlp_w1.astype(BF16), l1_mlp_w2.astype(BF16), final_norm)
    y_p, st1 = _layer1(x_p, mod1_p, *l1, ctx=None)
    y_s, _ = _layer1(x_s, mod1_s, *l1, ctx=(state_l1_ret_fwd, state_l1_ret_bwd))
    return (y_p, y_s) + st0 + st1
```

```python
import functools

import jax
import jax.numpy as jnp
import numpy as np
from jax import lax
from jax.experimental import pallas as pl
from jax.experimental.pallas import tpu as pltpu

F32 = jnp.float32
BF16 = jnp.bfloat16

LANES = 128
D_MODEL = 1024
GRID_W = 64
HEAD_DIM = 64
N_Q_HEADS = 8
N_KV_HEADS = 2
GQA = N_Q_HEADS // N_KV_HEADS
WINDOW = 128
Q_BLOCK = 128
ROPE_THETA = 10000.0
RET_DK = 256
RET_DV = 512
RET_HEADS = 4
D_FF = 4 * D_MODEL
EPS = 1e-6
NEG_INF = -1e30
ADA_CHUNKS = 6

ATTN_Q_W = N_Q_HEADS * HEAD_DIM
ATTN_KV_W = N_KV_HEADS * HEAD_DIM
ATTN_IN_W = 2 * ATTN_Q_W + 4 * ATTN_KV_W
QK_W = 2 * ATTN_Q_W + 2 * ATTN_KV_W
LOG2_E = 1.4426950408889634
Q_SCALE = HEAD_DIM ** -0.5 * LOG2_E
DENOM_ROWS = 16
RET_QK_W = RET_HEADS * RET_DK
RET_V_W = RET_HEADS * RET_DV

VMEM_LIMIT_BYTES = 56 * 1024 * 1024
ROW_TILE = 512
STREAM_ROWS = 256
ATTN_KEY_CHUNK = 256
KEY_SUB = 256
FULL_Q_BLOCKS = 1
BAND_Q_BLOCKS = 4
FF_CHUNK = 1024
RET_CHUNK = 256


def _resident(shape):
    n = len(shape)
    return pl.BlockSpec(shape, lambda *_: (0,) * n, pipeline_mode=pl.Buffered(1))


def _params(n_axes):
    return pltpu.CompilerParams(
        dimension_semantics=("arbitrary",) * n_axes,
        vmem_limit_bytes=VMEM_LIMIT_BYTES)


def _silu(x):
    return x / (1.0 + jnp.exp(-x))


def _rms(x, gain):
    return x * lax.rsqrt(jnp.mean(x * x, axis=-1, keepdims=True) + EPS) * gain


def _modulated_norm(x, gain, scale, shift):
    return _rms(x, gain) * (1.0 + scale) + shift


def _swap_halves(x, s):
    w = x.shape[-1]
    lane = lax.broadcasted_iota(jnp.int32, x.shape, x.ndim - 1)
    fwd = pltpu.roll(x, w - s, x.ndim - 1)
    bwd = pltpu.roll(x, s, x.ndim - 1)
    return jnp.where((lane & s) == 0, fwd, bwd)


def _rope(x, cos, sin_signed, s):
    reps = x.shape[-1] // cos.shape[-1]
    if reps > 1:
        cos = jnp.concatenate([cos] * reps, axis=-1)
        sin_signed = jnp.concatenate([sin_signed] * reps, axis=-1)
    return x * cos + _swap_halves(x, s) * sin_signed


def _rope_tables(seq_len, head_dim):
    n_heads = max(1, LANES // head_dim)
    pos = np.arange(seq_len)
    n_freq = head_dim // 4
    inv = ROPE_THETA ** (-np.arange(n_freq, dtype=np.float64) / n_freq)
    ang_r = (pos // GRID_W)[:, None] * inv[None, :]
    ang_c = (pos % GRID_W)[:, None] * inv[None, :]
    cos = np.concatenate([np.cos(ang_r)] * 2 + [np.cos(ang_c)] * 2, axis=-1)
    sin = np.concatenate([-np.sin(ang_r), np.sin(ang_r), -np.sin(ang_c), np.sin(ang_c)], axis=-1)
    return (jnp.asarray(np.tile(cos, (1, n_heads)), F32), jnp.asarray(np.tile(sin, (1, n_heads)), F32))


def _ada_body(cond_ref, w_ref, b_ref, o_ref):
    a = _silu(cond_ref[...]).astype(BF16)
    o_ref[...] = jnp.dot(a, w_ref[...].astype(BF16), preferred_element_type=F32) + b_ref[...]


def _ada(cond, w, b):
    rows = cond.shape[0]
    n = w.shape[1]
    return pl.pallas_call(
        _ada_body,
        grid=(n // D_MODEL,),
        in_specs=[pl.BlockSpec((rows, D_MODEL), lambda j: (0, 0)),
                  pl.BlockSpec((D_MODEL, D_MODEL), lambda j: (0, j)),
                  pl.BlockSpec((1, D_MODEL), lambda j: (0, j))],
        out_specs=pl.BlockSpec((rows, D_MODEL), lambda j: (0, j)),
        out_shape=jax.ShapeDtypeStruct((rows, n), F32),
        compiler_params=_params(1),
        name="ada_modulation",
    )(cond, w, b.reshape(1, n))


def _head_rms(x, avg, gain):
    sq = x * x
    hi = sq.astype(BF16)
    lo = (sq - hi.astype(F32)).astype(BF16)
    ms = (jnp.dot(hi, avg, preferred_element_type=F32)
          + jnp.dot(lo, avg, preferred_element_type=F32))
    return x * lax.rsqrt(ms + EPS) * gain


def _pre0_body(*refs, rope, emit_state):
    it = iter(refs)
    x_ref, mod_ref, gain_ref, w_ref, avg_ref, qg_ref, kg_ref = (next(it) for _ in range(7))
    cos_ref = next(it) if rope else None
    sin_ref = next(it) if rope else None
    qk_ref, vt_ref = next(it), next(it)
    state_ref = next(it) if emit_state else None

    streams = _row_streams(x_ref.shape[1])
    hs = [_modulated_norm(x_ref[0, rows, :], gain_ref[...], mod_ref[0, 1:2, :], mod_ref[0, 0:1, :]).astype(BF16)
          for rows in streams]
    base = ATTN_Q_W + 2 * ATTN_KV_W

    def proj(r, lo, width):
        return jnp.dot(hs[r], w_ref[:, lo:lo + width], preferred_element_type=F32)

    def rotate(t, rows):
        return _rope(t, cos_ref[rows, :], sin_ref[rows, :], HEAD_DIM // 4) if rope else t

    def emit_q(r, rows, lo, out_lo, normed):
        q = proj(r, lo, ATTN_Q_W)
        if normed:
            q = _head_rms(q, avg_ref[...], qg_ref[...])
        qk_ref[0, rows, out_lo:out_lo + ATTN_Q_W] = (rotate(q, rows) * Q_SCALE).astype(BF16)

    def emit_kv(r, rows, lo, group, normed):
        k = proj(r, lo, ATTN_KV_W)
        if normed:
            k = _head_rms(k, avg_ref[:ATTN_KV_W, :ATTN_KV_W], kg_ref[...])
        v = proj(r, lo + ATTN_KV_W, ATTN_KV_W)
        if emit_state:
            state_ref[0, rows, 2 * group * ATTN_KV_W:(2 * group + 2) * ATTN_KV_W] = (
                jnp.concatenate([k, v], axis=-1))
        out_lo = 2 * ATTN_Q_W + group * ATTN_KV_W
        qk_ref[0, rows, out_lo:out_lo + ATTN_KV_W] = rotate(k, rows).astype(BF16)
        vt = v.T.astype(BF16)
        for j in range(N_KV_HEADS):
            vt_ref[0, group * N_KV_HEADS + j, :, rows] = vt[j * HEAD_DIM:(j + 1) * HEAD_DIM, :]

    for r, rows in enumerate(streams):
        emit_q(r, rows, 0, 0, True)
    for r, rows in enumerate(streams):
        emit_kv(r, rows, ATTN_Q_W, 0, True)
    for r, rows in enumerate(streams):
        emit_q(r, rows, base, ATTN_Q_W, False)
    for r, rows in enumerate(streams):
        emit_kv(r, rows, base + ATTN_Q_W, 1, False)


def _pre0(x, mod, gain, w_in, q_gain, k_gain, *, rope, emit_state):
    bsz, seq_len, _ = x.shape
    tm = min(ROW_TILE, seq_len)
    per_batch_mod = mod.shape[0] > 1
    head = np.arange(ATTN_Q_W) // HEAD_DIM
    avg = jnp.asarray(np.where(head[:, None] == head[None, :], 1.0 / HEAD_DIM, 0.0), BF16)
    inputs = [x, mod, gain.reshape(1, D_MODEL), w_in, avg,
              jnp.tile(q_gain, N_Q_HEADS).reshape(1, ATTN_Q_W),
              jnp.tile(k_gain, N_KV_HEADS).reshape(1, ATTN_KV_W)]
    in_specs = [pl.BlockSpec((1, tm, D_MODEL), lambda i, b: (b, i, 0)),
                pl.BlockSpec((1, ADA_CHUNKS, D_MODEL),
                             (lambda i, b: (b, 0, 0)) if per_batch_mod else (lambda i, b: (0, 0, 0))),
                _resident((1, D_MODEL)), _resident((D_MODEL, ATTN_IN_W)),
                _resident((ATTN_Q_W, ATTN_Q_W)), _resident((1, ATTN_Q_W)), _resident((1, ATTN_KV_W))]
    if rope:
        cos, sin = _rope_tables(seq_len, HEAD_DIM)
        inputs += [cos, sin]
        in_specs += [pl.BlockSpec((tm, cos.shape[1]), lambda i, b: (i, 0))] * 2
    out_shape = [jax.ShapeDtypeStruct((bsz, seq_len, QK_W), BF16),
                 jax.ShapeDtypeStruct((bsz, 2 * N_KV_HEADS, HEAD_DIM, seq_len), BF16)]
    out_specs = [pl.BlockSpec((1, tm, QK_W), lambda i, b: (b, i, 0)),
                 pl.BlockSpec((1, 2 * N_KV_HEADS, HEAD_DIM, tm), lambda i, b: (b, 0, 0, i))]
    if emit_state:
        out_shape.append(jax.ShapeDtypeStruct((bsz, seq_len, 4 * ATTN_KV_W), F32))
        out_specs.append(pl.BlockSpec((1, tm, 4 * ATTN_KV_W), lambda i, b: (b, i, 0)))
    return pl.pallas_call(
        functools.partial(_pre0_body, rope=rope, emit_state=emit_state),
        grid=(seq_len // tm, bsz),
        in_specs=in_specs, out_specs=out_specs, out_shape=out_shape,
        compiler_params=_params(2),
        name="l0_in_proj",
    )(*inputs)


def _attn_body(*refs, seq_len, q_blocks, has_ctx, band, has_sink):
    it = iter(refs)
    sink_ref = next(it) if has_sink else None
    q_ref, k_ref, vt_ref = next(it), next(it), next(it)
    ck_ref = next(it) if has_ctx else None
    cvt_ref = next(it) if has_ctx else None
    bias_ref = next(it) if band else None
    o_ref = next(it)

    n_blocks = q_blocks
    tq = Q_BLOCK
    cols = GQA * tq
    streams = [(j, h) for j in range(n_blocks) for h in range(N_KV_HEADS)]

    qt = q_ref[0].astype(F32).T
    qpt = {}
    for j, h in streams:
        rows = [qt[(h * GQA + g) * HEAD_DIM:(h * GQA + g + 1) * HEAD_DIM, j * tq:(j + 1) * tq]
                for g in range(GQA)]
        qh = jnp.concatenate(rows, axis=1)
        halves = [jnp.zeros_like(qh)] * N_KV_HEADS
        halves[h] = qh
        qpt[j, h] = jnp.concatenate(halves, axis=0).astype(BF16)

    m, acc = {}, {}
    for j, h in streams:
        if has_sink:
            group = lax.broadcasted_iota(jnp.int32, (1, cols), 1) // tq
            m_h = jnp.zeros((1, cols), F32)
            for g in range(GQA):
                m_h = jnp.where(group == g, sink_ref[h * GQA + g] * LOG2_E, m_h)
            denom0 = 1.0
        else:
            m_h = jnp.full((1, cols), NEG_INF, F32)
            denom0 = 0.0
        m[j, h] = m_h
        acc[j, h] = jnp.concatenate([jnp.zeros((HEAD_DIM, cols), F32),
                                     jnp.full((DENOM_ROWS, cols), denom0, F32)], axis=0)

    def static_subs(k_at, vt_at, lo, n_keys):
        return [(KEY_SUB, functools.partial(k_at, lo + o), functools.partial(vt_at, lo + o), None)
                for o in range(0, n_keys, KEY_SUB)]

    shared = []
    if has_ctx:
        past = ck_ref.shape[1]
        tk = min(ATTN_KEY_CHUNK, past)
        for c in range(past // tk):
            shared.append(static_subs(lambda o: ck_ref[0, o:o + KEY_SUB, :],
                                      lambda o, h: cvt_ref[0, h, :, o:o + KEY_SUB], c * tk, tk))
    if not band:
        tk = min(ATTN_KEY_CHUNK, seq_len)
        for c in range(seq_len // tk):
            shared.append(static_subs(lambda o: k_ref[0, o:o + KEY_SUB, :],
                                      lambda o, h: vt_ref[0, h, :, o:o + KEY_SUB], c * tk, tk))
    chunks = {j: list(shared) for j in range(n_blocks)}
    if band:
        width = Q_BLOCK + 2 * WINDOW
        for j in range(n_blocks):
            qi = pl.program_id(1) * n_blocks + j
            start = jnp.clip(qi * tq - WINDOW, 0, seq_len - width)
            variant = jnp.where(qi == 0, 0, jnp.where(qi == seq_len // tq - 1, 2, 1))

            def window_rows(o, n, start=start):
                return pl.ds(pl.multiple_of(start + o, Q_BLOCK), n)

            for o in range(0, width, KEY_SUB):
                n = min(KEY_SUB, width - o)
                chunks[j].append([(
                    n,
                    functools.partial(lambda o, n, rows: k_ref[0, rows(o, n), :], o, n, window_rows),
                    functools.partial(lambda o, n, rows, h: vt_ref[0, h, :, rows(o, n)], o, n, window_rows),
                    functools.partial(lambda o, n, v: bias_ref[v, o:o + n, :], o, n, variant))])
    n_chunks = len(chunks[0])

    def scores(sub, key):
        _, keys, _, bias = sub
        st = jnp.dot(keys(), qpt[key], preferred_element_type=F32)
        return st if bias is None else st + bias()

    def col_max(a, b):
        b = jnp.max(b, axis=0, keepdims=True)
        return b if a is None else jnp.maximum(a, b)

    def rescale(key, chunk_max):
        m_new = jnp.maximum(m[key], chunk_max)
        acc[key] = acc[key] * jnp.exp2(m[key] - m_new)
        m[key] = m_new

    cur = {key: [] for key in streams}
    chunk_max = {key: None for key in streams}
    for s in range(len(chunks[0][0])):
        for key in streams:
            cur[key].append(scores(chunks[key[0]][0][s], key))
            chunk_max[key] = col_max(chunk_max[key], cur[key][-1])
    for key in streams:
        rescale(key, chunk_max[key])
    for c in range(n_chunks):
        n_cur = len(chunks[0][c])
        n_next = len(chunks[0][c + 1]) if c + 1 < n_chunks else 0
        nxt = {key: [] for key in streams}
        chunk_max = {key: None for key in streams}
        for s in range(max(n_cur, n_next)):
            for key in streams:
                j, h = key
                if s < n_next:
                    nxt[key].append(scores(chunks[j][c + 1][s], key))
                    chunk_max[key] = col_max(chunk_max[key], nxt[key][-1])
                if s < n_cur:
                    n_keys, _, values_t, _ = chunks[j][c][s]
                    pt = jnp.exp2(cur[key][s] - m[key]).astype(BF16)
                    lhs = jnp.concatenate([values_t(h), jnp.ones((DENOM_ROWS, n_keys), BF16)], axis=0)
                    acc[key] = acc[key] + jnp.dot(lhs, pt, preferred_element_type=F32)
        if n_next:
            for key in streams:
                rescale(key, chunk_max[key])
            cur = nxt

    for j in range(n_blocks):
        pieces = []
        for h in range(N_KV_HEADS):
            out_t = acc[j, h][:HEAD_DIM, :] / acc[j, h][HEAD_DIM:HEAD_DIM + 1, :]
            pieces += [out_t[:, g * tq:(g + 1) * tq] for g in range(GQA)]
        o_ref[0, j * tq:(j + 1) * tq, :] = jnp.concatenate(pieces, axis=0).T.astype(o_ref.dtype)


def _band_bias():
    width = Q_BLOCK + 2 * WINDOW
    r = np.arange(width)[:, None]
    t = (np.arange(GQA * Q_BLOCK) % Q_BLOCK)[None, :]
    rel = np.stack([r - t, r - t - WINDOW, r - t - 2 * WINDOW])
    return jnp.asarray(np.where(np.abs(rel) <= WINDOW, 0.0, NEG_INF), F32)


def _attention(qk, vt, group, *, q_blocks, ctx=None, sink=None, band=False):
    bsz, seq_len, _ = qk.shape
    k_col = 2 * ATTN_Q_W // ATTN_KV_W + group
    inputs, in_specs = [], []
    if sink is not None:
        inputs.append(sink)
        in_specs.append(pl.BlockSpec(memory_space=pltpu.SMEM))
    inputs += [qk, qk, vt]
    tq = q_blocks * Q_BLOCK
    in_specs += [pl.BlockSpec((1, tq, ATTN_Q_W), lambda b, i: (b, i, group)),
                 pl.BlockSpec((1, seq_len, ATTN_KV_W), lambda b, i: (b, 0, k_col)),
                 pl.BlockSpec((1, N_KV_HEADS, HEAD_DIM, seq_len), lambda b, i: (b, group, 0, 0))]
    if ctx is not None:
        past = ctx[0].shape[1]
        inputs += list(ctx)
        in_specs += [pl.BlockSpec((1, past, ATTN_KV_W), lambda b, i: (b, 0, 0)),
                     pl.BlockSpec((1, N_KV_HEADS, HEAD_DIM, past), lambda b, i: (b, 0, 0, 0))]
    if band:
        assert seq_len >= Q_BLOCK + 2 * WINDOW
        bias = _band_bias()
        inputs.append(bias)
        in_specs.append(_resident(bias.shape))
    return pl.pallas_call(
        functools.partial(_attn_body, seq_len=seq_len, q_blocks=q_blocks, has_ctx=ctx is not None,
                          band=band, has_sink=sink is not None),
        grid=(bsz, seq_len // tq),
        in_specs=in_specs,
        out_specs=pl.BlockSpec((1, tq, ATTN_Q_W), lambda b, i: (b, i, 0)),
        out_shape=jax.ShapeDtypeStruct((bsz, seq_len, ATTN_Q_W), BF16),
        compiler_params=_params(2),
        name="l0_attention_band" if band else "l0_attention_full",
    )(*inputs)


def _row_streams(n_rows):
    if n_rows % (2 * STREAM_ROWS) == 0:
        return [slice(lo, lo + n_rows // 2) for lo in (0, n_rows // 2)]
    return [slice(0, n_rows)]


def _mlp_tail(xs, ys, mod_ref, nmlp_ref, w1_ref, w2_ref, fin_ref):
    xs = [x + mod_ref[0, 2:3, :] * y for x, y in zip(xs, ys)]
    hs = [_modulated_norm(x, nmlp_ref[...], mod_ref[0, 4:5, :], mod_ref[0, 3:4, :]).astype(BF16)
          for x in xs]
    ffs = [jnp.zeros_like(x) for x in xs]
    for j in range(D_FF // FF_CHUNK):
        cols = slice(j * FF_CHUNK, (j + 1) * FF_CHUNK)
        for r, h in enumerate(hs):
            hid = jnp.maximum(jnp.dot(h, w1_ref[:, cols], preferred_element_type=F32), 0.0)
            ffs[r] = ffs[r] + jnp.dot((hid * hid).astype(BF16), w2_ref[cols, :],
                                      preferred_element_type=F32)
    xs = [x + mod_ref[0, 5:6, :] * ff for x, ff in zip(xs, ffs)]
    if fin_ref is not None:
        xs = [_rms(x, fin_ref[...]) for x in xs]
    return xs


def _post0_body(*refs, final):
    it = iter(refs)
    x_ref, oa_ref, os_ref, mod_ref, wo_ref, nmlp_ref, w1_ref, w2_ref = (next(it) for _ in range(8))
    fin_ref = next(it) if final else None
    o_ref = next(it)
    streams = _row_streams(x_ref.shape[1])
    ys = [jnp.dot(oa_ref[0, rows, :], wo_ref[:ATTN_Q_W, :], preferred_element_type=F32)
          + jnp.dot(os_ref[0, rows, :], wo_ref[ATTN_Q_W:, :], preferred_element_type=F32)
          for rows in streams]
    outs = _mlp_tail([x_ref[0, rows, :] for rows in streams], ys, mod_ref, nmlp_ref, w1_ref, w2_ref, fin_ref)
    for rows, out in zip(streams, outs):
        o_ref[0, rows, :] = out


def _post1_body(*refs, final):
    it = iter(refs)
    x_ref, o_in_ref, g_ref, gn_ref, mod_ref, wo_ref, nmlp_ref, w1_ref, w2_ref = (next(it) for _ in range(9))
    fin_ref = next(it) if final else None
    o_ref = next(it)
    streams = _row_streams(x_ref.shape[1])
    ys = [jnp.zeros((rows.stop - rows.start, D_MODEL), F32) for rows in streams]
    for h in range(RET_HEADS):
        cols = slice(h * RET_DV, (h + 1) * RET_DV)
        for r, rows in enumerate(streams):
            normed = _rms(o_in_ref[0, rows, cols], gn_ref[:, cols])
            gated = (_silu(g_ref[0, rows, cols].astype(F32)) * normed).astype(BF16)
            ys[r] = ys[r] + jnp.dot(gated, wo_ref[cols, :], preferred_element_type=F32)
    outs = _mlp_tail([x_ref[0, rows, :] for rows in streams], ys, mod_ref, nmlp_ref, w1_ref, w2_ref, fin_ref)
    for rows, out in zip(streams, outs):
        o_ref[0, rows, :] = out


def _post(body, x, mixer_inputs, mixer_specs, mod, w_out, norm_mlp, w1, w2, final_gain, name):
    bsz, seq_len, _ = x.shape
    tm = min(ROW_TILE, seq_len)
    per_batch_mod = mod.shape[0] > 1
    row_spec = pl.BlockSpec((1, tm, D_MODEL), lambda i, b: (b, i, 0))
    inputs = [x] + list(mixer_inputs) + [mod, w_out, norm_mlp.reshape(1, D_MODEL), w1, w2]
    in_specs = [row_spec] + list(mixer_specs(tm)) + [
        pl.BlockSpec((1, ADA_CHUNKS, D_MODEL),
                     (lambda i, b: (b, 0, 0)) if per_batch_mod else (lambda i, b: (0, 0, 0))),
        _resident(w_out.shape), _resident((1, D_MODEL)), _resident(w1.shape), _resident(w2.shape)]
    if final_gain is not None:
        inputs.append(final_gain.reshape(1, D_MODEL))
        in_specs.append(_resident((1, D_MODEL)))
    return pl.pallas_call(
        functools.partial(body, final=final_gain is not None),
        grid=(seq_len // tm, bsz),
        in_specs=in_specs, out_specs=row_spec,
        out_shape=jax.ShapeDtypeStruct(x.shape, F32),
        compiler_params=_params(2),
        name=name,
    )(*inputs)


def _post0(x, oa, os_, mod, w_out, norm_mlp, w1, w2):
    def specs(tm):
        return [pl.BlockSpec((1, tm, ATTN_Q_W), lambda i, b: (b, i, 0))] * 2
    return _post(_post0_body, x, [oa, os_], specs, mod, w_out, norm_mlp, w1, w2, None, "l0_out_proj_mlp")


def _post1(x, o_ret, g, gn_gain, mod, w_out, norm_mlp, w1, w2, final_gain):
    def specs(tm):
        return [pl.BlockSpec((1, tm, RET_V_W), lambda i, b: (b, i, 0)),
                pl.BlockSpec((1, tm, RET_V_W), lambda i, b: (b, i, 0)),
                _resident((1, RET_V_W))]
    return _post(_post1_body, x, [o_ret, g, gn_gain.reshape(1, RET_V_W)], specs, mod, w_out, norm_mlp,
                 w1, w2, final_gain, "l1_out_proj_mlp")


def _pre1_body(*refs, rope, chunk):
    it = iter(refs)
    x_ref, mod_ref, gain_ref, w_ref = (next(it) for _ in range(4))
    cos_ref = next(it) if rope else None
    sin_ref = next(it) if rope else None
    q_ref, kt_ref, v_ref, g_ref = (next(it) for _ in range(4))

    streams = _row_streams(x_ref.shape[1])
    hs = [_modulated_norm(x_ref[0, rows, :], gain_ref[...], mod_ref[0, 1:2, :], mod_ref[0, 0:1, :]).astype(BF16)
          for rows in streams]

    def proj(r, lo):
        return jnp.dot(hs[r], w_ref[:, lo:lo + RET_QK_W], preferred_element_type=F32)

    def rotate(t, rows):
        return _rope(t, cos_ref[rows, :], sin_ref[rows, :], RET_DK // 4) if rope else t

    def emit_q(r, rows):
        q_ref[0, rows, :] = rotate(proj(r, 0), rows).astype(BF16)

    def emit_k(r, rows):
        k = rotate(proj(r, RET_QK_W), rows) * (RET_DK ** -0.5)
        for lo in range(rows.start, rows.stop, chunk):
            kt_ref[0, lo // chunk] = k[lo - rows.start:lo - rows.start + chunk, :].T.astype(BF16)

    def emit_wide(r, rows, ref, lo, j):
        cols = slice(j * RET_QK_W, (j + 1) * RET_QK_W)
        ref[0, rows, cols] = proj(r, lo + j * RET_QK_W).astype(BF16)

    sections = [emit_q, emit_k]
    for j in range(RET_V_W // RET_QK_W):
        sections.append(functools.partial(emit_wide, ref=v_ref, lo=2 * RET_QK_W, j=j))
        sections.append(functools.partial(emit_wide, ref=g_ref, lo=2 * RET_QK_W + RET_V_W, j=j))
    for section in sections:
        for r, rows in enumerate(streams):
            section(r, rows)


def _pre1(x, mod, gain, w_in, *, rope, chunk):
    bsz, seq_len, _ = x.shape
    tm = min(ROW_TILE, seq_len)
    per_batch_mod = mod.shape[0] > 1
    inputs = [x, mod, gain.reshape(1, D_MODEL), w_in]
    in_specs = [pl.BlockSpec((1, tm, D_MODEL), lambda i, b: (b, i, 0)),
                pl.BlockSpec((1, ADA_CHUNKS, D_MODEL),
                             (lambda i, b: (b, 0, 0)) if per_batch_mod else (lambda i, b: (0, 0, 0))),
                _resident((1, D_MODEL)), _resident(w_in.shape)]
    if rope:
        cos, sin = _rope_tables(seq_len, RET_DK)
        inputs += [cos, sin]
        in_specs += [pl.BlockSpec((tm, cos.shape[1]), lambda i, b: (i, 0))] * 2
    out_shape = [jax.ShapeDtypeStruct((bsz, seq_len, RET_QK_W), BF16),
                 jax.ShapeDtypeStruct((bsz, seq_len // chunk, RET_QK_W, chunk), BF16),
                 jax.ShapeDtypeStruct((bsz, seq_len, RET_V_W), BF16),
                 jax.ShapeDtypeStruct((bsz, seq_len, RET_V_W), BF16)]
    out_specs = [pl.BlockSpec((1, tm, RET_QK_W), lambda i, b: (b, i, 0)),
                 pl.BlockSpec((1, tm // chunk, RET_QK_W, chunk), lambda i, b: (b, i, 0, 0)),
                 pl.BlockSpec((1, tm, RET_V_W), lambda i, b: (b, i, 0)),
                 pl.BlockSpec((1, tm, RET_V_W), lambda i, b: (b, i, 0))]
    return pl.pallas_call(
        functools.partial(_pre1_body, rope=rope, chunk=chunk),
        grid=(seq_len // tm, bsz),
        in_specs=in_specs, out_specs=out_specs, out_shape=out_shape,
        compiler_params=_params(2),
        name="l1_in_proj",
    )(*inputs)


def _retention_body(*refs, n_chunks, chunk, has_init, emit_state):
    it = iter(refs)
    lgf_ref, lgb_ref, q_ref, kt_ref, v_ref = (next(it) for _ in range(5))
    s0f_ref = next(it) if has_init else None
    s0b_ref = next(it) if has_init else None
    o_ref = next(it)
    sf_out = next(it) if emit_state else None
    sb_out = next(it) if emit_state else None
    sf_ref, sb_ref = next(it), next(it)

    head = pl.program_id(1)
    lg_f = lgf_ref[head]
    lg_b = lgb_ref[head]
    c_f32 = float(chunk)

    i_idx = lax.broadcasted_iota(jnp.int32, (chunk, chunk), 0).astype(F32)
    j_idx = lax.broadcasted_iota(jnp.int32, (chunk, chunk), 1).astype(F32)
    diff = i_idx - j_idx
    decay = (jnp.exp(jnp.where(diff >= 0, diff * lg_f, NEG_INF))
             + jnp.exp(jnp.where(diff <= 0, -diff * lg_b, NEG_INF)))
    row = lax.broadcasted_iota(jnp.int32, (chunk, 1), 0).astype(F32)
    lanes = lax.broadcasted_iota(jnp.int32, (1, chunk), 1).astype(F32)
    q_decay_f = jnp.exp((row + 1.0) * lg_f)
    q_decay_b = jnp.exp((c_f32 - row) * lg_b)
    k_decay_f = jnp.exp((c_f32 - 1.0 - lanes) * lg_f)
    k_decay_b = jnp.exp(lanes * lg_b)
    one = jnp.ones((1, 1), F32)
    chunk_decay_f = jnp.exp(one * (c_f32 * lg_f))
    chunk_decay_b = jnp.exp(one * (c_f32 * lg_b))

    if has_init:
        sf_ref[...] = s0f_ref[0, 0]
        sb_ref[...] = s0b_ref[0, 0]
    else:
        sf_ref[...] = jnp.zeros_like(sf_ref)
        sb_ref[...] = jnp.zeros_like(sb_ref)

    def rows_of(c):
        return pl.ds(pl.multiple_of(c * chunk, chunk), chunk)

    def visit(c, s_ref, q_decay, k_decay, chunk_decay, first):
        qc = q_ref[0, rows_of(c), :]
        ktc = kt_ref[0, c]
        vc = v_ref[0, rows_of(c), :]
        inter = jnp.dot(qc, s_ref[...].astype(BF16), preferred_element_type=F32) * q_decay
        if first:
            att = jnp.dot(qc, ktc, preferred_element_type=F32) * decay
            o_ref[0, rows_of(c), :] = jnp.dot(att.astype(BF16), vc, preferred_element_type=F32) + inter
        else:
            o_ref[0, rows_of(c), :] = o_ref[0, rows_of(c), :] + inter
        s_ref[...] = chunk_decay * s_ref[...] + jnp.dot(
            (ktc.astype(F32) * k_decay).astype(BF16), vc, preferred_element_type=F32)

    def visit_pair(c_up, c_down, first):
        visit(c_up, sf_ref, q_decay_f, k_decay_f, chunk_decay_f, first)
        visit(c_down, sb_ref, q_decay_b, k_decay_b, chunk_decay_b, first)

    half = n_chunks // 2
    if n_chunks % 2:
        raise NotImplementedError("retention kernel needs an even number of chunks")

    def first_half(t, _):
        visit_pair(t, n_chunks - 1 - t, True)
        return 0

    def second_half(t, _):
        visit_pair(half + t, half - 1 - t, False)
        return 0

    lax.fori_loop(0, half, first_half, 0, unroll=True)
    lax.fori_loop(0, half, second_half, 0, unroll=True)
    if emit_state:
        sf_out[0, 0] = sf_ref[...]
        sb_out[0, 0] = sb_ref[...]


def _retention(q, kt, v, lg_f, lg_b, init, *, chunk, emit_state):
    bsz, seq_len, _ = q.shape
    n_chunks = seq_len // chunk
    smem = pl.BlockSpec(memory_space=pltpu.SMEM)
    state_spec = pl.BlockSpec((1, 1, RET_DK, RET_DV), lambda b, h: (b, h, 0, 0))
    inputs = [lg_f, lg_b, q, kt, v]
    in_specs = [smem, smem,
                pl.BlockSpec((1, seq_len, RET_DK), lambda b, h: (b, 0, h)),
                pl.BlockSpec((1, n_chunks, RET_DK, chunk), lambda b, h: (b, 0, h, 0)),
                pl.BlockSpec((1, seq_len, RET_DV), lambda b, h: (b, 0, h))]
    if init is not None:
        inputs += list(init)
        in_specs += [state_spec, state_spec]
    out_shape = [jax.ShapeDtypeStruct((bsz, seq_len, RET_V_W), F32)]
    out_specs = [pl.BlockSpec((1, seq_len, RET_DV), lambda b, h: (b, 0, h))]
    if emit_state:
        out_shape += [jax.ShapeDtypeStruct((bsz, RET_HEADS, RET_DK, RET_DV), F32)] * 2
        out_specs += [state_spec, state_spec]
    return pl.pallas_call(
        functools.partial(_retention_body, n_chunks=n_chunks, chunk=chunk,
                          has_init=init is not None, emit_state=emit_state),
        grid=(bsz, RET_HEADS),
        in_specs=in_specs, out_specs=out_specs, out_shape=out_shape,
        scratch_shapes=[pltpu.VMEM((RET_DK, RET_DV), F32), pltpu.VMEM((RET_DK, RET_DV), F32)],
        compiler_params=_params(2),
        name="l1_retention",
    )(*inputs)


def _layer0(x, mod, norm_mix, norm_mlp, w_in, q_gain, k_gain, sink, w_out, w1, w2, ctx):
    sample = ctx is not None
    outs = _pre0(x, mod, norm_mix, w_in, q_gain, k_gain, rope=sample, emit_state=not sample)
    qk, vt = outs[0], outs[1]
    ctx_a = ctx_s = None
    if sample:
        bsz, past = ctx[0].shape[:2]

        def keys(t):
            return t.reshape(bsz, past, ATTN_KV_W).astype(BF16)

        def values_t(t):
            return jnp.transpose(t, (0, 2, 3, 1)).astype(BF16)

        ctx_a = (keys(ctx[0]), values_t(ctx[1]))
        ctx_s = (keys(ctx[2]), values_t(ctx[3]))
    out_a = _attention(qk, vt, 0, q_blocks=FULL_Q_BLOCKS, ctx=ctx_a)
    out_s = _attention(qk, vt, 1, q_blocks=BAND_Q_BLOCKS if sample else FULL_Q_BLOCKS,
                       ctx=ctx_s, sink=sink, band=sample)
    x = _post0(x, out_a, out_s, mod, w_out, norm_mlp, w1, w2)
    state = ()
    if not sample:
        bsz, seq_len, _ = x.shape
        st = outs[2].reshape(bsz, seq_len, 4, N_KV_HEADS, HEAD_DIM)
        state = tuple(st[:, :, j] for j in range(4))
    return x, state


def _layer1(x, mod, norm_mix, norm_mlp, w_in, lg_f, lg_b, gn_gain, w_out, w1, w2, final_gain, ctx):
    sample = ctx is not None
    seq_len = x.shape[1]
    chunk = RET_CHUNK if seq_len % (2 * RET_CHUNK) == 0 else RET_CHUNK // 2
    q, kt, v, g = _pre1(x, mod, norm_mix, w_in, rope=sample, chunk=chunk)
    outs = _retention(q, kt, v, lg_f, lg_b, ctx, chunk=chunk, emit_state=not sample)
    x = _post1(x, outs[0], g, gn_gain, mod, w_out, norm_mlp, w1, w2, final_gain)
    return x, tuple(outs[1:])


def kernel(x_prompt, x_sample, c, cache_l0_attn_k, cache_l0_attn_v, cache_l0_swa_k, cache_l0_swa_v, state_l1_ret_fwd, state_l1_ret_bwd, c_ctx, l0_ada_w, l0_ada_b, l0_norm_mix, l0_norm_mlp, l0_w_in, l0_q_norm, l0_k_norm, l0_sink, l0_w_out, l0_mlp_w1, l0_mlp_w2, l1_ada_w, l1_ada_b, l1_norm_mix, l1_norm_mlp, l1_w_in, l1_ret_decay_fwd, l1_ret_decay_bwd, l1_ret_gn, l1_w_out, l1_mlp_w1, l1_mlp_w2, final_norm):
    n_dec = c.shape[0]
    pad = (-(n_dec + 1)) % 8
    cond = jnp.concatenate([c, c_ctx[None, :], jnp.zeros((pad, D_MODEL), F32)], axis=0)

    def modulation(w, b):
        mod = _ada(cond, w, b).reshape(cond.shape[0], ADA_CHUNKS, D_MODEL)
        return mod[n_dec:n_dec + 1], mod[:n_dec]

    mod0_p, mod0_s = modulation(l0_ada_w, l0_ada_b)
    mod1_p, mod1_s = modulation(l1_ada_w, l1_ada_b)

    l0 = (l0_norm_mix, l0_norm_mlp, l0_w_in.astype(BF16), l0_q_norm, l0_k_norm, l0_sink,
          l0_w_out.astype(BF16), l0_mlp_w1.astype(BF16), l0_mlp_w2.astype(BF16))
    x_p, st0 = _layer0(x_prompt, mod0_p, *l0, ctx=None)
    x_s, _ = _layer0(x_sample, mod0_s, *l0,
                     ctx=(cache_l0_attn_k, cache_l0_attn_v, cache_l0_swa_k, cache_l0_swa_v))

    l1 = (l1_norm_mix, l1_norm_mlp, l1_w_in.astype(BF16),
          jax.nn.log_sigmoid(l1_ret_decay_fwd.astype(F32)), jax.nn.log_sigmoid(l1_ret_decay_bwd.astype(F32)),
          l1_ret_gn, l1_w_out.astype(BF16), l1_mlp_w1.astype(BF16), l1_mlp_w2.astype(BF16), final_norm)
    y_p, st1 = _layer1(x_p, mod1_p, *l1, ctx=None)
    y_s, _ = _layer1(x_s, mod1_s, *l1, ctx=(state_l1_ret_fwd, state_l1_ret_bwd))
    return (y_p, y_s) + st0 + st1
```
